```python
import jax, jax.numpy as jnp
from jax import lax
import numpy as np

D_MODEL = 1024
BATCH = 8
SEQ = 8192
DEPTH = 2

GRID_W = 64
CTX_LEN = 256
N_BRANCH = 3
GLA_HEADS = 4
GLA_DK = 64
GLA_DV = 128
GLA_GATE_RANK = 16
GLA_GATE_NORM = 16.0
GLA_CHUNK = 64
MLA_HEADS = 8
MLA_Q_LORA = 256
MLA_KV_LORA = 128
MLA_NOPE = 64
MLA_ROPE = 32
MLA_DV = 64
ROPE_THETA = 10000.0
ROPE_FREQS = MLA_ROPE // 4
Q_BLOCK = 128
SC_W = 512
D_FF = 2816
NORM_EPS = 1e-6
IN_SIZES = (GLA_HEADS * GLA_DK, GLA_HEADS * GLA_DK, GLA_HEADS * GLA_DV, GLA_HEADS * GLA_DV,
            2 * GLA_GATE_RANK, MLA_Q_LORA, MLA_KV_LORA, MLA_ROPE, SC_W, SC_W, SC_W,
            N_BRANCH * D_MODEL)
IN_W = sum(IN_SIZES)

kernel_name = 'hybrid_gla_mla_shortconv_dit'


def rms_norm(x, g):
    xf = x.astype(jnp.float32)
    y = xf * lax.rsqrt(jnp.mean(xf * xf, axis=-1, keepdims=True) + NORM_EPS)
    return (y * g).astype(x.dtype)


def modulate(h, shift, scale):
    return h * (1.0 + scale) + shift


def flip(t):
    return jnp.flip(t, axis=1)


def split_proj(p):
    idx = np.cumsum(IN_SIZES)[:-1].tolist()
    return jnp.split(p, idx, axis=-1)


def dwconv3(u, w):
    L = u.shape[1]
    up = jnp.pad(u, ((0, 0), (1, 1), (0, 0)))
    return up[:, 0:L] * w[0] + up[:, 1:L + 1] * w[1] + up[:, 2:L + 2] * w[2]


def axial_rope(rows):
    row = jnp.repeat(jnp.arange(rows, dtype=jnp.float32), GRID_W)
    col = jnp.tile(jnp.arange(GRID_W, dtype=jnp.float32), rows)
    inv = ROPE_THETA ** (-jnp.arange(ROPE_FREQS, dtype=jnp.float32) / ROPE_FREQS)
    ang = jnp.stack([row[:, None] * inv, col[:, None] * inv], axis=1)
    return jnp.cos(ang), jnp.sin(ang)


def apply_rope(x, cos, sin):
    xr = x.reshape(x.shape[:-1] + (2, 2, ROPE_FREQS)).astype(jnp.float32)
    x0, x1 = xr[..., 0, :], xr[..., 1, :]
    out = jnp.stack([x0 * cos - x1 * sin, x1 * cos + x0 * sin], axis=-2)
    return out.reshape(x.shape).astype(x.dtype)


def gla_heads(t, dh):
    return t.reshape(t.shape[0], t.shape[1], GLA_HEADS, dh)


def gla_log_decay(lr, w2, b2):
    z = (lr @ w2 + b2).astype(jnp.float32)
    return gla_heads(jax.nn.log_sigmoid(z) / GLA_GATE_NORM, GLA_DK)


def gla_chunked(q, k, v, log_a, s0):
    B, L, H, DK = q.shape
    DV = v.shape[-1]
    nc = L // GLA_CHUNK
    to_chunks = lambda t: t.reshape(B, nc, GLA_CHUNK, H, t.shape[-1])
    qc, kc, vc = to_chunks(q), to_chunks(k), to_chunks(v)
    b = jnp.cumsum(to_chunks(log_a).astype(jnp.float32), axis=2)
    b_last = b[:, :, -1:]
    q_in = qc * jnp.exp(b)
    k_in = kc * jnp.exp(-b)
    k_st = kc * jnp.exp(b_last - b)
    lower = jnp.tril(jnp.ones((GLA_CHUNK, GLA_CHUNK), dtype=bool))
    a_intra = jnp.where(lower, jnp.einsum('bnihd,bnjhd->bnhij', q_in, k_in), 0.0)
    o_intra = jnp.einsum('bnhij,bnjhv->bnihv', a_intra, vc)
    u = jnp.einsum('bnjhd,bnjhv->bnhdv', k_st, vc)
    decay = jnp.exp(b_last[:, :, 0])

    def step(s, xs):
        q_n, d_n, u_n = xs
        o_n = jnp.einsum('bihd,bhdv->bihv', q_n, s)
        return d_n[..., None] * s + u_n, o_n

    s_fin, o_inter = lax.scan(step, s0.astype(jnp.float32),
                              (q_in.swapaxes(0, 1), decay.swapaxes(0, 1), u.swapaxes(0, 1)))
    o = o_intra + o_inter.swapaxes(0, 1)
    return o.reshape(B, L, H, DV).astype(v.dtype), s_fin


def gla_final_state(k, v, log_a):
    b = jnp.cumsum(log_a.astype(jnp.float32), axis=1)
    return jnp.einsum('blhd,blhv->bhdv', k * jnp.exp(b[:, -1:] - b), v)


def gla_branch(q_a, k_a, v_a, r_a, alr, lp, s_f0, s_b0):
    B, L = q_a.shape[:2]
    q = gla_heads(q_a, GLA_DK) * GLA_DK ** -0.5
    k = gla_heads(k_a, GLA_DK)
    v = gla_heads(v_a, GLA_DV)
    lr_f, lr_b = jnp.split(alr, 2, axis=-1)
    la_f = gla_log_decay(lr_f, lp['w_gk2'][0], lp['b_gk2'][0])
    la_b = gla_log_decay(lr_b, lp['w_gk2'][1], lp['b_gk2'][1])
    o_f, _ = gla_chunked(q, k, v, la_f, s_f0)
    o_b, _ = gla_chunked(flip(q), flip(k), flip(v), flip(la_b), s_b0)
    o = rms_norm(o_f + flip(o_b), lp['gla_norm_g'].reshape(GLA_HEADS, GLA_DV))
    return (o.reshape(B, L, GLA_HEADS * GLA_DV) * jax.nn.silu(r_a)) @ lp['w_br_a']


def mla_q(cq, lp, rope):
    B, L = cq.shape[:2]
    q = (rms_norm(cq, lp['mla_q_norm_g']) @ lp['w_uq']).reshape(B, L, MLA_HEADS, MLA_NOPE + MLA_ROPE)
    if rope is not None:
        q_rope = apply_rope(q[..., MLA_NOPE:], rope[0][:, None], rope[1][:, None])
        q = jnp.concatenate([q[..., :MLA_NOPE], q_rope], axis=-1)
    return q * (MLA_NOPE + MLA_ROPE) ** -0.5


def mla_kv(ckv, kr, lp, rope):
    B, L = ckv.shape[:2]
    kv = (rms_norm(ckv, lp['mla_kv_norm_g']) @ lp['w_ukv']).reshape(B, L, MLA_HEADS, MLA_NOPE + MLA_DV)
    k_nope, v = kv[..., :MLA_NOPE], kv[..., MLA_NOPE:]
    if rope is not None:
        kr = apply_rope(kr, rope[0], rope[1])
    k_rope = jnp.broadcast_to(kr[:, :, None, :], (B, L, MLA_HEADS, MLA_ROPE))
    return jnp.concatenate([k_nope, k_rope], axis=-1), v


def softmax_attend(q, k, v):
    s = jnp.einsum('bqhd,bkhd->bhqk', q, k).astype(jnp.float32)
    p = jax.nn.softmax(s, axis=-1).astype(v.dtype)
    return jnp.einsum('bhqk,bkhv->bqhv', p, v)


def blocked_attend(q, k, v):
    B, L, H, Dq = q.shape
    qb = q.reshape(B, L // Q_BLOCK, Q_BLOCK, H, Dq).swapaxes(0, 1)
    o = lax.map(lambda qq: softmax_attend(qq, k, v), qb)
    return o.swapaxes(0, 1).reshape(B, L, H, v.shape[-1])


def short_conv_branch(sb, sc, sx, lp):
    return (sb * dwconv3(sc * sx, lp['sc_w'])) @ lp['w_br_c']


def merge_branches(y_a, y_b, y_c, gates, lp):
    g_a, g_b, g_c = jnp.split(jax.nn.sigmoid(gates), N_BRANCH, axis=-1)
    return (g_a * y_a + g_b * y_b + g_c * y_c) @ lp['w_o']


def context_memory(proj_c, lp):
    _, k_a, v_a, _, alr, _, ckv, kr, _, _, _, _ = split_proj(proj_c)
    k = gla_heads(k_a, GLA_DK)
    v = gla_heads(v_a, GLA_DV)
    lr_f, lr_b = jnp.split(alr, 2, axis=-1)
    s_f = gla_final_state(k, v, gla_log_decay(lr_f, lp['w_gk2'][0], lp['b_gk2'][0]))
    s_b = gla_final_state(flip(k), flip(v), flip(gla_log_decay(lr_b, lp['w_gk2'][1], lp['b_gk2'][1])))
    k_m, v_m = mla_kv(ckv, kr, lp, None)
    return (k_m, v_m, s_f, s_b)


def latent_mixers(proj, lp, rope, mem):
    k_ctx, v_ctx, s_f, s_b = mem
    q_a, k_a, v_a, r_a, alr, cq, ckv, kr, sb, sc, sx, gates = split_proj(proj)
    y_a = gla_branch(q_a, k_a, v_a, r_a, alr, lp, s_f, s_b)
    q = mla_q(cq, lp, rope)
    k, v = mla_kv(ckv, kr, lp, rope)
    o = blocked_attend(q, jnp.concatenate([k, k_ctx], axis=1), jnp.concatenate([v, v_ctx], axis=1))
    y_b = o.reshape(o.shape[0], o.shape[1], MLA_HEADS * MLA_DV) @ lp['w_br_b']
    y_c = short_conv_branch(sb, sc, sx, lp)
    return merge_branches(y_a, y_b, y_c, gates, lp)


def context_mixers(proj_c, lp, mem):
    k_ctx, v_ctx, _, _ = mem
    q_a, k_a, v_a, r_a, alr, cq, _, _, sb, sc, sx, gates = split_proj(proj_c)
    zero = jnp.zeros((proj_c.shape[0], GLA_HEADS, GLA_DK, GLA_DV), jnp.float32)
    y_a = gla_branch(q_a, k_a, v_a, r_a, alr, lp, zero, zero)
    o = softmax_attend(mla_q(cq, lp, None), k_ctx, v_ctx)
    y_b = o.reshape(o.shape[0], o.shape[1], MLA_HEADS * MLA_DV) @ lp['w_br_b']
    y_c = short_conv_branch(sb, sc, sx, lp)
    return merge_branches(y_a, y_b, y_c, gates, lp)


def conv_ffn(h, lp):
    g = dwconv3(h @ lp['w_ffn_gate'], lp['ffn_conv_w']) + lp['ffn_conv_b']
    return (jax.nn.silu(g) * (h @ lp['w_ffn_up'])) @ lp['w_ffn_down']


def setup_inputs(seed: int = 0) -> dict:
    key = jax.random.key(seed)
    ks = iter(jax.random.split(key, 32))

    def nrm(shape, scale):
        return jax.random.normal(next(ks), shape, jnp.float32) * scale

    def gain(shape):
        return 1.0 + nrm(shape, 0.05)

    L = DEPTH
    D = D_MODEL
    return {
        'x': nrm((BATCH, SEQ, D), 1.0),
        'c': nrm((BATCH, D), 1.0),
        'ctx': nrm((BATCH, CTX_LEN, D), 1.0),
        'c_ctx': nrm((D,), 1.0),
        'w_ada': nrm((L, D, 6 * D), 0.5 * D ** -0.5),
        'b_ada': nrm((L, 6 * D), 0.02),
        'norm1_g': gain((L, D)),
        'w_in': nrm((L, D, IN_W), D ** -0.5),
        'w_gk2': nrm((L, 2, GLA_GATE_RANK, GLA_HEADS * GLA_DK), GLA_GATE_RANK ** -0.5),
        'b_gk2': nrm((L, 2, GLA_HEADS * GLA_DK), 0.1),
        'gla_norm_g': gain((L, GLA_HEADS * GLA_DV)),
        'mla_q_norm_g': gain((L, MLA_Q_LORA)),
        'w_uq': nrm((L, MLA_Q_LORA, MLA_HEADS * (MLA_NOPE + MLA_ROPE)), MLA_Q_LORA ** -0.5),
        'mla_kv_norm_g': gain((L, MLA_KV_LORA)),
        'w_ukv': nrm((L, MLA_KV_LORA, MLA_HEADS * (MLA_NOPE + MLA_DV)), MLA_KV_LORA ** -0.5),
        'sc_w': nrm((L, 3, SC_W), 3 ** -0.5),
        'w_br_a': nrm((L, GLA_HEADS * GLA_DV, D), (GLA_HEADS * GLA_DV) ** -0.5),
        'w_br_b': nrm((L, MLA_HEADS * MLA_DV, D), (MLA_HEADS * MLA_DV) ** -0.5),
        'w_br_c': nrm((L, SC_W, D), SC_W ** -0.5),
        'w_o': nrm((L, D, D), D ** -0.5),
        'norm2_g': gain((L, D)),
        'w_ffn_gate': nrm((L, D, D_FF), D ** -0.5),
        'w_ffn_up': nrm((L, D, D_FF), D ** -0.5),
        'ffn_conv_w': nrm((L, 3, D_FF), 3 ** -0.5),
        'ffn_conv_b': nrm((L, D_FF), 0.02),
        'w_ffn_down': nrm((L, D_FF, D), D_FF ** -0.5),
        'final_norm_g': gain((D,)),
    }


def reference(x, c, ctx, c_ctx, w_ada, b_ada, norm1_g, w_in, w_gk2, b_gk2, gla_norm_g,
              mla_q_norm_g, w_uq, mla_kv_norm_g, w_ukv, sc_w, w_br_a, w_br_b, w_br_c, w_o,
              norm2_g, w_ffn_gate, w_ffn_up, ffn_conv_w, ffn_conv_b, w_ffn_down, final_norm_g):
    rows = x.shape[1] // GRID_W
    rope = axial_rope(rows)
    xc = ctx
    for l in range(DEPTH):
        last = l == DEPTH - 1
        lp = {
            'w_gk2': w_gk2[l], 'b_gk2': b_gk2[l], 'gla_norm_g': gla_norm_g[l],
            'mla_q_norm_g': mla_q_norm_g[l], 'w_uq': w_uq[l],
            'mla_kv_norm_g': mla_kv_norm_g[l], 'w_ukv': w_ukv[l], 'sc_w': sc_w[l],
            'w_br_a': w_br_a[l], 'w_br_b': w_br_b[l], 'w_br_c': w_br_c[l], 'w_o': w_o[l],
            'w_ffn_gate': w_ffn_gate[l], 'w_ffn_up': w_ffn_up[l], 'ffn_conv_w': ffn_conv_w[l],
            'ffn_conv_b': ffn_conv_b[l], 'w_ffn_down': w_ffn_down[l],
        }
        mod = jax.nn.silu(c) @ w_ada[l] + b_ada[l]
        mod_c = jax.nn.silu(c_ctx) @ w_ada[l] + b_ada[l]
        sh1, sc1, g1, sh2, sc2, g2 = jnp.split(mod[:, None, :], 6, axis=-1)
        sh1c, sc1c, g1c, sh2c, sc2c, g2c = jnp.split(mod_c, 6, axis=-1)
        proj_c = modulate(rms_norm(xc, norm1_g[l]), sh1c, sc1c) @ w_in[l]
        mem = context_memory(proj_c, lp)
        proj = modulate(rms_norm(x, norm1_g[l]), sh1, sc1) @ w_in[l]
        x = x + g1 * latent_mixers(proj, lp, rope, mem)
        if not last:
            xc = xc + g1c * context_mixers(proj_c, lp, mem)
        x = x + g2 * conv_ffn(modulate(rms_norm(x, norm2_g[l]), sh2, sc2), lp)
        if not last:
            xc = xc + g2c * conv_ffn(modulate(rms_norm(xc, norm2_g[l]), sh2c, sc2c), lp)
    return rms_norm(x, final_norm_g)
```

```python
import functools

import numpy as np
import jax
import jax.numpy as jnp
from jax import lax
from jax.experimental import pallas as pl
from jax.experimental.pallas import tpu as pltpu

F32 = jnp.float32
BF = jnp.bfloat16

GRID_W = 64
GLA_HEADS = 4
GLA_DK = 64
GLA_DV = 128
GLA_RANK = 16
GLA_GATE_NORM = 16.0
GLA_CHUNK = 64
MLA_HEADS = 8
MLA_Q_LORA = 256
MLA_KV_LORA = 128
MLA_NOPE = 64
MLA_ROPE = 32
MLA_DV = 64
ROPE_THETA = 10000.0
ROPE_FREQS = MLA_ROPE // 4
SC_W = 512
NORM_EPS = 1e-6
N_BRANCH = 3

QK_W = GLA_HEADS * GLA_DK
V_W = GLA_HEADS * GLA_DV
Q_ABS = MLA_KV_LORA + MLA_ROPE
PAIR = 2 * GLA_CHUNK
HALO = 8
NEG_BIG = -1e30

VMEM_LIMIT = 56 * 1024 * 1024


def _dot(a, b):
    return jnp.dot(a, b, preferred_element_type=F32)


def _dot_nt(a, b):
    return lax.dot_general(a, b, (((1,), (1,)), ((), ())), preferred_element_type=F32)


def _rms(x, g):
    return x * lax.rsqrt(jnp.mean(x * x, axis=-1, keepdims=True) + NORM_EPS) * g


def _silu(x):
    return x * jax.nn.sigmoid(x)


def _const_spec(shape):
    nd = len(shape)
    return pl.BlockSpec(shape, lambda *_: (0,) * nd, pipeline_mode=pl.Buffered(1))


def _params(sem):
    return pltpu.CompilerParams(dimension_semantics=sem, vmem_limit_bytes=VMEM_LIMIT)


def _mod_kernel(c_ref, w_ref, b_ref, o_ref):
    a = _silu(c_ref[...]).astype(BF)
    o_ref[0] = _dot(a, w_ref[0].astype(BF)) + b_ref[0]


def _mod_call(cc, w_ada, b_ada):
    nl, d, n = w_ada.shape
    rows = cc.shape[0]
    tn = 1024
    return pl.pallas_call(
        _mod_kernel,
        grid=(nl, n // tn),
        in_specs=[
            pl.BlockSpec((rows, d), lambda l, j: (0, 0)),
            pl.BlockSpec((1, d, tn), lambda l, j: (l, 0, j)),
            pl.BlockSpec((1, 1, tn), lambda l, j: (l, 0, j)),
        ],
        out_specs=pl.BlockSpec((1, rows, tn), lambda l, j: (l, 0, j)),
        out_shape=jax.ShapeDtypeStruct((nl, rows, n), F32),
        compiler_params=_params(("parallel", "parallel")),
        name="adaln_mod",
    )(cc, w_ada, b_ada.reshape(nl, 1, n))


def _proj_kernel(x_ref, mod_ref, g1_ref, wqkv_ref, wsm1_ref, wsm2_ref, wgk_ref, bgk_ref,
                 wcq_ref, gq_ref, wuqn_ref, wuk_ref, wuqr_ref, wuqrr_ref, wckv_ref, gkv_ref,
                 ws_ref, cosq_ref, sinq_ref, cosk_ref, sink_ref,
                 qk_ref, v_ref, vt_ref, la_ref, qt_ref, kabs_ref, ct_ref, u_ref):
    x = x_ref[0]
    m = mod_ref[0]
    h = _rms(x, g1_ref[...]) * (1.0 + m[1:2]) + m[0:1]
    hb = h.astype(BF)

    qkv = _dot(hb, wqkv_ref[...])
    qk_ref[0] = qkv[:, :2 * QK_W].astype(BF)
    v = qkv[:, 2 * QK_W:]
    v_ref[0] = v.astype(BF)
    vt_ref[0] = v.T.astype(BF)

    sm1 = _dot(hb, wsm1_ref[...])
    sm2 = _dot(hb, wsm2_ref[...])
    z = _dot(sm1.astype(BF), wgk_ref[...]) + bgk_ref[...]
    la_ref[0] = jax.nn.log_sigmoid(z) * (1.0 / GLA_GATE_NORM)

    scale = (MLA_NOPE + MLA_ROPE) ** -0.5
    cqn = _rms(_dot(hb, wcq_ref[...]), gq_ref[...]).astype(BF)
    qnt = _dot_nt(wuqn_ref[...], cqn).astype(BF)
    qrope_t = (_dot_nt(wuqr_ref[...], cqn) * cosq_ref[...]
               + _dot_nt(wuqrr_ref[...], cqn) * sinq_ref[...]) * scale
    for hh in range(MLA_HEADS):
        qa = _dot(wuk_ref[hh], qnt[MLA_NOPE * hh:MLA_NOPE * (hh + 1), :]) * scale
        qt_ref[0, Q_ABS * hh:Q_ABS * hh + MLA_KV_LORA, :] = qa.astype(BF)
        qt_ref[0, Q_ABS * hh + MLA_KV_LORA:Q_ABS * (hh + 1), :] = (
            qrope_t[MLA_ROPE * hh:MLA_ROPE * (hh + 1), :].astype(BF))

    cn = _rms(_dot(hb, wckv_ref[...]), gkv_ref[...])
    kabs_ref[0, :, 0:MLA_KV_LORA] = cn.astype(BF)
    kr = sm1 * cosk_ref[...] + sm2 * sink_ref[...]
    kabs_ref[0, :, MLA_KV_LORA:Q_ABS] = kr[:, 0:MLA_ROPE].astype(BF)
    ct_ref[0] = cn.T.astype(BF)

    s = _dot(hb, ws_ref[...])
    u_ref[0] = s[:, :SC_W] * s[:, SC_W:]


def _proj_call(x, mod, lw, tabs, tm):
    b, l, d = x.shape
    cosq, sinq, cosk, sink = tabs
    tok = lambda c: pl.BlockSpec((1, tm, c), lambda bi, i: (bi, i, 0))
    tr = lambda r: pl.BlockSpec((1, r, tm), lambda bi, i: (bi, 0, i))
    weights = [lw["g1"], lw["wqkv"], lw["wsm1"], lw["wsm2"], lw["wgk"], lw["bgk"], lw["wcq"],
               lw["gq"], lw["wuqn"], lw["wuk"], lw["wuqr"], lw["wuqrr"], lw["wckv"], lw["gkv"],
               lw["ws"]]
    in_specs = ([tok(d), pl.BlockSpec((1, 6, d), lambda bi, i: (bi, 0, 0))]
                + [_const_spec(w.shape) for w in weights]
                + [pl.BlockSpec((MLA_HEADS * MLA_ROPE, tm), lambda bi, i: (0, i)),
                   pl.BlockSpec((MLA_HEADS * MLA_ROPE, tm), lambda bi, i: (0, i)),
                   pl.BlockSpec((tm, 128), lambda bi, i: (i, 0)),
                   pl.BlockSpec((tm, 128), lambda bi, i: (i, 0))])
    out_shape = (
        jax.ShapeDtypeStruct((b, l, 2 * QK_W), BF),
        jax.ShapeDtypeStruct((b, l, V_W), BF),
        jax.ShapeDtypeStruct((b, V_W, l), BF),
        jax.ShapeDtypeStruct((b, l, 2 * QK_W), F32),
        jax.ShapeDtypeStruct((b, MLA_HEADS * Q_ABS, l), BF),
        jax.ShapeDtypeStruct((b, l, Q_ABS), BF),
        jax.ShapeDtypeStruct((b, MLA_KV_LORA, l), BF),
        jax.ShapeDtypeStruct((b, l, SC_W), F32),
    )
    out_specs = (tok(2 * QK_W), tok(V_W), tr(V_W), tok(2 * QK_W), tr(MLA_HEADS * Q_ABS),
                 tok(Q_ABS), tr(MLA_KV_LORA), tok(SC_W))
    return pl.pallas_call(
        _proj_kernel,
        grid=(b, l // tm),
        in_specs=in_specs,
        out_specs=out_specs,
        out_shape=out_shape,
        compiler_params=_params(("parallel", "parallel")),
        name="proj",
    )(x, mod, *weights, cosq, sinq, cosk, sink)


def _gla_kernel(qkf_ref, qkb_ref, vf_ref, vb_ref, vtf_ref, vtb_ref, laf_ref, lab_ref, s0_ref,
                of_ref, ob_ref, sfin_ref, s_ref, *, tb):
    n = pl.program_id(1)
    npairs = tb // PAIR

    @pl.when(n == 0)
    def _():
        s_ref[...] = s0_ref[0]

    r = lax.broadcasted_iota(jnp.int32, (PAIR, PAIR), 0)
    c = lax.broadcasted_iota(jnp.int32, (PAIR, PAIR), 1)
    same = (r // GLA_CHUNK) == (c // GLA_CHUNK)
    ones_c = jnp.where(same, 1.0, 0.0).astype(BF)
    lane_head = lax.broadcasted_iota(jnp.int32, (PAIR, QK_W), 1) // GLA_DK
    row_chunk = lax.broadcasted_iota(jnp.int32, (PAIR, QK_W), 0) // GLA_CHUNK
    bd_mask = (lax.broadcasted_iota(jnp.int32, (V_W, QK_W), 0) // GLA_DV
               == lax.broadcasted_iota(jnp.int32, (V_W, QK_W), 1) // GLA_DK)

    def pair_dir(qk_ref, v_ref, vt_ref, la_ref, o_ref, sidx, p, forward):
        start = pl.multiple_of(p * PAIR, PAIR)
        rows = pl.ds(start, PAIR)
        tri = jnp.logical_and(same, (c <= r) if forward else (c >= r))
        tri_b = jnp.where(tri, 1.0, 0.0).astype(BF)
        la = la_ref[0, rows, :]
        hi = la.astype(BF)
        lo = (la - hi.astype(F32)).astype(BF)
        bcum = _dot(tri_b, hi) + _dot(tri_b, lo)
        blast = _dot(ones_c, hi) + _dot(ones_c, lo)
        qk = qk_ref[0, rows, :].astype(F32)
        q = qk[:, :QK_W] * (GLA_DK ** -0.5)
        k = qk[:, QK_W:]
        q_in = (q * jnp.exp(bcum)).astype(BF)
        k_in = (k * jnp.exp(-bcum)).astype(BF)
        k_st = (k * jnp.exp(blast - bcum)).astype(BF)
        decay = jnp.exp(blast)
        qs = jnp.concatenate(
            [jnp.where(lane_head == hh, q_in, jnp.zeros_like(q_in)) for hh in range(GLA_HEADS)], axis=0)
        a = _dot_nt(qs, k_in)
        tri4 = jnp.concatenate([tri] * GLA_HEADS, axis=0)
        a = jnp.where(tri4, a, 0.0).astype(BF)
        v = v_ref[0, rows, :]
        o_intra = jnp.concatenate(
            [_dot(a[PAIR * hh:PAIR * (hh + 1)], v[:, GLA_DV * hh:GLA_DV * (hh + 1)])
             for hh in range(GLA_HEADS)], axis=1)
        vt = vt_ref[0, :, rows]
        for ci in ((0, 1) if forward else (1, 0)):
            lo_r, hi_r = GLA_CHUNK * ci, GLA_CHUNK * (ci + 1)
            st = s_ref[sidx]
            o_inter = _dot_nt(q_in[lo_r:hi_r], st.astype(BF))
            o_ref[0, pl.ds(start + lo_r, GLA_CHUNK), :] = o_intra[lo_r:hi_r] + o_inter
            k_c = jnp.where(row_chunk == ci, k_st, jnp.zeros_like(k_st))
            ut = _dot(vt, k_c)
            s_ref[sidx] = st * decay[lo_r:lo_r + 1, :] + jnp.where(bd_mask, ut, 0.0)

    def body(p, carry):
        pair_dir(qkf_ref, vf_ref, vtf_ref, laf_ref, of_ref, 0, p, True)
        pair_dir(qkb_ref, vb_ref, vtb_ref, lab_ref, ob_ref, 1, npairs - 1 - p, False)
        return carry

    lax.fori_loop(0, npairs, body, 0)

    @pl.when(n == pl.num_programs(1) - 1)
    def _():
        sfin_ref[0] = s_ref[...]


def _gla_call(qk, v, vt, la, s0, tb):
    b, l, _ = qk.shape
    nb = l // tb
    fwd = lambda c: pl.BlockSpec((1, tb, c), lambda bi, i: (bi, i, 0))
    bwd = lambda c: pl.BlockSpec((1, tb, c), lambda bi, i: (bi, nb - 1 - i, 0))
    st_spec = pl.BlockSpec((1, 2, V_W, QK_W), lambda bi, i: (bi, 0, 0, 0))
    return pl.pallas_call(
        functools.partial(_gla_kernel, tb=tb),
        grid=(b, nb),
        in_specs=[fwd(2 * QK_W), bwd(2 * QK_W), fwd(V_W), bwd(V_W),
                  pl.BlockSpec((1, V_W, tb), lambda bi, i: (bi, 0, i)),
                  pl.BlockSpec((1, V_W, tb), lambda bi, i: (bi, 0, nb - 1 - i)),
                  pl.BlockSpec((1, tb, QK_W), lambda bi, i: (bi, i, 0)),
                  pl.BlockSpec((1, tb, QK_W), lambda bi, i: (bi, nb - 1 - i, 1)),
                  st_spec],
        out_specs=(fwd(V_W), bwd(V_W), st_spec),
        out_shape=(jax.ShapeDtypeStruct((b, l, V_W), F32),
                   jax.ShapeDtypeStruct((b, l, V_W), F32),
                   jax.ShapeDtypeStruct((b, 2, V_W, QK_W), F32)),
        scratch_shapes=[pltpu.VMEM((2, V_W, QK_W), F32)],
        compiler_params=_params(("parallel", "arbitrary")),
        name="gla",
    )(qk, qk, v, v, vt, vt, la, la, s0)


def _attn_kernel(qt_ref, k_ref, ct_ref, wuv_ref, o_ref, ot_ref, *, tk):
    tq = qt_ref.shape[2]
    nchunks = k_ref.shape[1] // tk

    def head_body(hh, carry):
        qh = qt_ref[0, pl.ds(pl.multiple_of(hh * Q_ABS, 16), Q_ABS), :]

        def chunk_body(j, st):
            m, l, acc = st
            ks = k_ref[0, pl.ds(pl.multiple_of(j * tk, tk), tk), :]
            s = _dot(ks, qh)
            m_new = jnp.maximum(m, jnp.max(s, axis=0, keepdims=True))
            alpha = jnp.exp(m - m_new)
            p = jnp.exp(s - m_new)
            l_new = alpha * l + jnp.sum(p, axis=0, keepdims=True)
            cc = ct_ref[0, :, pl.ds(pl.multiple_of(j * tk, tk), tk)]
            acc_new = alpha * acc + _dot(cc, p.astype(BF))
            return m_new, l_new, acc_new

        init = (jnp.full((1, tq), NEG_BIG, F32), jnp.zeros((1, tq), F32),
                jnp.zeros((MLA_KV_LORA, tq), F32))
        _, l, acc = lax.fori_loop(0, nchunks, chunk_body, init)
        o_lat = (acc / l).astype(BF)
        ot_ref[pl.ds(pl.multiple_of(hh * MLA_DV, MLA_DV), MLA_DV), :] = _dot(wuv_ref[hh], o_lat)
        return carry

    lax.fori_loop(0, MLA_HEADS, head_body, 0)
    o_ref[0] = ot_ref[...].T.astype(BF)


def _attn_call(qt, kabs, ct, wuv, tq, tk):
    b, _, l = qt.shape
    lk = kabs.shape[1]
    return pl.pallas_call(
        functools.partial(_attn_kernel, tk=tk),
        grid=(b, l // tq),
        in_specs=[pl.BlockSpec((1, MLA_HEADS * Q_ABS, tq), lambda bi, i: (bi, 0, i)),
                  pl.BlockSpec((1, lk, Q_ABS), lambda bi, i: (bi, 0, 0)),
                  pl.BlockSpec((1, MLA_KV_LORA, lk), lambda bi, i: (bi, 0, 0)),
                  _const_spec(wuv.shape)],
        out_specs=pl.BlockSpec((1, tq, MLA_HEADS * MLA_DV), lambda bi, i: (bi, i, 0)),
        out_shape=jax.ShapeDtypeStruct((b, l, MLA_HEADS * MLA_DV), BF),
        scratch_shapes=[pltpu.VMEM((MLA_HEADS * MLA_DV, tq), F32)],
        compiler_params=_params(("parallel", "parallel")),
        name="attn",
    )(qt, kabs, ct, wuv)


def _shift_rows(t, prev_row, next_row):
    n = t.shape[0]
    row = lax.broadcasted_iota(jnp.int32, t.shape, 0)
    down = jnp.where(row == 0, prev_row, pltpu.roll(t, 1, 0))
    up = jnp.where(row == n - 1, next_row, pltpu.roll(t, n - 1, 0))
    return down, up


def _merge_kernel(x_ref, mod_ref, g1_ref, of_ref, ob_ref, oat_ref, u_ref, up_ref, un_ref,
                  wrsg_ref, glag_ref, wbra_ref, wbrb_ref, wbrc_ref, scw_ref, wo_ref, xo_ref):
    i = pl.program_id(1)
    x = x_ref[0]
    m = mod_ref[0]
    d = x.shape[1]
    hb = (_rms(x, g1_ref[...]) * (1.0 + m[1:2]) + m[0:1]).astype(BF)
    rsg = _dot(hb, wrsg_ref[...])
    r_a = rsg[:, :V_W]
    sb = rsg[:, V_W:V_W + SC_W]
    gates = jax.nn.sigmoid(rsg[:, V_W + SC_W:])

    o = of_ref[0] + ob_ref[0]
    gg = glag_ref[...]
    on = jnp.concatenate(
        [_rms(o[:, GLA_DV * hh:GLA_DV * (hh + 1)], gg[:, GLA_DV * hh:GLA_DV * (hh + 1)])
         for hh in range(GLA_HEADS)], axis=1)
    y_a = _dot((on * _silu(r_a)).astype(BF), wbra_ref[...])
    y_b = _dot(oat_ref[0], wbrb_ref[...])

    u = u_ref[0]
    prev_row = jnp.where(i > 0, up_ref[0, HALO - 1:HALO, :], 0.0)
    next_row = jnp.where(i < pl.num_programs(1) - 1, un_ref[0, 0:1, :], 0.0)
    u_dn, u_up = _shift_rows(u, prev_row, next_row)
    w = scw_ref[...]
    conv = u_dn * w[0:1] + u * w[1:2] + u_up * w[2:3]
    y_c = _dot((sb * conv).astype(BF), wbrc_ref[...])

    mix = gates[:, :d] * y_a + gates[:, d:2 * d] * y_b + gates[:, 2 * d:] * y_c
    xo_ref[0] = x + m[2:3] * _dot(mix.astype(BF), wo_ref[...])


def _halo_specs(tm, l, c):
    nb8 = tm // HALO
    last8 = l // HALO - 1
    prev = pl.BlockSpec((1, HALO, c), lambda bi, i: (bi, jnp.maximum(i * nb8 - 1, 0), 0))
    nxt = pl.BlockSpec((1, HALO, c), lambda bi, i: (bi, jnp.minimum((i + 1) * nb8, last8), 0))
    return prev, nxt


def _merge_call(x, mod, o_f, o_b, o_att, u, lw, tm):
    b, l, d = x.shape
    tok = lambda c: pl.BlockSpec((1, tm, c), lambda bi, i: (bi, i, 0))
    up_spec, un_spec = _halo_specs(tm, l, SC_W)
    weights = [lw["wrsg"], lw["glag"], lw["wbra"], lw["wbrb"], lw["wbrc"], lw["scw"], lw["wo"]]
    return pl.pallas_call(
        _merge_kernel,
        grid=(b, l // tm),
        in_specs=[tok(d), pl.BlockSpec((1, 6, d), lambda bi, i: (bi, 0, 0)), _const_spec(lw["g1"].shape),
                  tok(V_W), tok(V_W), tok(MLA_HEADS * MLA_DV), tok(SC_W), up_spec, un_spec]
                 + [_const_spec(w.shape) for w in weights],
        out_specs=tok(d),
        out_shape=jax.ShapeDtypeStruct((b, l, d), F32),
        compiler_params=_params(("parallel", "parallel")),
        name="merge",
    )(x, mod, lw["g1"], o_f, o_b, o_att, u, u, u, *weights)


def _ffn_kernel(x_ref, xp_ref, xn_ref, mod_ref, g2_ref, wg_ref, wu_ref, cw_ref, wd_ref, gf_ref,
                xo_ref, *, fc, final):
    i = pl.program_id(1)
    x = x_ref[0]
    m = mod_ref[0]
    tm = x.shape[0]
    nf = wg_ref.shape[1]

    def hmod(t):
        return _rms(t, g2_ref[...]) * (1.0 + m[4:5]) + m[3:4]

    h = hmod(x)
    hb = h.astype(BF)
    h_ext = jnp.concatenate([hmod(xp_ref[0]), h, hmod(xn_ref[0])], axis=0).astype(BF)
    has_prev = i > 0
    has_next = i < pl.num_programs(1) - 1
    row = lax.broadcasted_iota(jnp.int32, (tm, fc), 0)

    acc = jnp.zeros((tm, x.shape[1]), F32)
    for ci in range(nf // fc):
        cols = slice(ci * fc, (ci + 1) * fc)
        g_ext = _dot(h_ext, wg_ref[:, cols])
        g_mid = g_ext[HALO:HALO + tm]
        g_dn = g_ext[HALO - 1:HALO - 1 + tm]
        g_up = g_ext[HALO + 1:HALO + 1 + tm]
        g_dn = jnp.where(jnp.logical_and(row == 0, jnp.logical_not(has_prev)), 0.0, g_dn)
        g_up = jnp.where(jnp.logical_and(row == tm - 1, jnp.logical_not(has_next)), 0.0, g_up)
        cw = cw_ref[:, cols]
        g = g_dn * cw[0:1] + g_mid * cw[1:2] + g_up * cw[2:3] + cw[3:4]
        a = (_silu(g) * _dot(hb, wu_ref[:, cols])).astype(BF)
        acc = acc + _dot(a, wd_ref[cols, :])
    y = x + m[5:6] * acc
    if final:
        y = _rms(y, gf_ref[...])
    xo_ref[0] = y


def _ffn_call(x, mod, lw, gfinal, tm, final):
    b, l, d = x.shape
    nf = lw["wg"].shape[1]
    fc = 256
    tok = pl.BlockSpec((1, tm, d), lambda bi, i: (bi, i, 0))
    xp_spec, xn_spec = _halo_specs(tm, l, d)
    weights = [lw["g2"], lw["wg"], lw["wu"], lw["cw"], lw["wd"], gfinal]
    return pl.pallas_call(
        functools.partial(_ffn_kernel, fc=fc, final=final),
        grid=(b, l // tm),
        in_specs=[tok, xp_spec, xn_spec, pl.BlockSpec((1, 6, d), lambda bi, i: (bi, 0, 0))]
                 + [_const_spec(w.shape) for w in weights],
        out_specs=tok,
        out_shape=jax.ShapeDtypeStruct((b, l, d), F32),
        compiler_params=_params(("parallel", "parallel")),
        name="ffn",
    )(x, x, x, mod, *weights)


def _rope_perm():
    idx = np.zeros((MLA_ROPE,), np.int32)
    sgn = np.zeros((MLA_ROPE,), np.float32)
    for ax in range(2):
        for f in range(ROPE_FREQS):
            lo = ax * 2 * ROPE_FREQS + f
            hi = lo + ROPE_FREQS
            idx[lo], sgn[lo] = hi, -1.0
            idx[hi], sgn[hi] = lo, 1.0
    return idx, sgn


def _rope_tables(l, rotate):
    if rotate:
        rows = l // GRID_W
        row = jnp.repeat(jnp.arange(rows, dtype=F32), GRID_W)
        col = jnp.tile(jnp.arange(GRID_W, dtype=F32), rows)
        inv = ROPE_THETA ** (-jnp.arange(ROPE_FREQS, dtype=F32) / ROPE_FREQS)
        ang = jnp.stack([row[:, None] * inv, col[:, None] * inv], axis=1)
        cos, sin = jnp.cos(ang), jnp.sin(ang)
        expand = lambda t: jnp.broadcast_to(t[:, :, None, :], (l, 2, 2, ROPE_FREQS)).reshape(l, MLA_ROPE)
        cos32, sin32 = expand(cos), expand(sin)
    else:
        cos32, sin32 = jnp.ones((l, MLA_ROPE), F32), jnp.zeros((l, MLA_ROPE), F32)
    cosq = jnp.tile(cos32, (1, MLA_HEADS)).T
    sinq = jnp.tile(sin32, (1, MLA_HEADS)).T
    pad = lambda t: jnp.pad(t, ((0, 0), (0, 128 - MLA_ROPE)))
    return cosq, sinq, pad(cos32), pad(sin32)


def _layer_weights(l, p):
    d = p["w_in"].shape[1]
    w_in = p["w_in"][l]
    sizes = (QK_W, QK_W, V_W, V_W, 2 * GLA_RANK, MLA_Q_LORA, MLA_KV_LORA, MLA_ROPE, SC_W, SC_W, SC_W,
             N_BRANCH * d)
    offs = np.concatenate([[0], np.cumsum(sizes)])
    col = lambda a, b: w_in[:, int(offs[a]):int(offs[b])]
    idx, sgn = _rope_perm()
    w_kr = col(7, 8)
    w_kr_rot = w_kr[:, idx] * sgn
    zeros = lambda n: jnp.zeros((d, n), F32)
    wgk = jnp.zeros((128, 2 * QK_W), F32)
    wgk = wgk.at[MLA_ROPE:MLA_ROPE + GLA_RANK, :QK_W].set(p["w_gk2"][l, 0])
    wgk = wgk.at[MLA_ROPE + GLA_RANK:MLA_ROPE + 2 * GLA_RANK, QK_W:].set(p["w_gk2"][l, 1])
    w_uq = p["w_uq"][l].reshape(MLA_Q_LORA, MLA_HEADS, MLA_NOPE + MLA_ROPE)
    w_uq_rope = w_uq[:, :, MLA_NOPE:]
    w_ukv = p["w_ukv"][l].reshape(MLA_KV_LORA, MLA_HEADS, MLA_NOPE + MLA_DV)
    nf = p["w_ffn_gate"].shape[2]
    cw = jnp.concatenate([p["ffn_conv_w"][l], p["ffn_conv_b"][l][None], jnp.zeros((4, nf), F32)], axis=0)
    scw = jnp.concatenate([p["sc_w"][l], jnp.zeros((5, SC_W), F32)], axis=0)
    bf = lambda t: t.astype(BF)
    return {
        "g1": p["norm1_g"][l][None],
        "wqkv": bf(col(0, 3)),
        "wsm1": bf(jnp.concatenate([w_kr, col(4, 5), zeros(128 - MLA_ROPE - 2 * GLA_RANK)], axis=1)),
        "wsm2": bf(jnp.concatenate([w_kr_rot, zeros(128 - MLA_ROPE)], axis=1)),
        "wgk": bf(wgk),
        "bgk": p["b_gk2"][l].reshape(1, 2 * QK_W),
        "wcq": bf(col(5, 6)),
        "gq": p["mla_q_norm_g"][l][None],
        "wuqn": bf(w_uq[:, :, :MLA_NOPE].reshape(MLA_Q_LORA, -1).T),
        "wuk": bf(jnp.transpose(w_ukv[:, :, :MLA_NOPE], (1, 0, 2))),
        "wuqr": bf(w_uq_rope.reshape(MLA_Q_LORA, -1).T),
        "wuqrr": bf((w_uq_rope[:, :, idx] * sgn).reshape(MLA_Q_LORA, -1).T),
        "wckv": bf(col(6, 7)),
        "gkv": p["mla_kv_norm_g"][l][None],
        "ws": bf(col(9, 11)),
        "wuv": bf(jnp.transpose(w_ukv[:, :, MLA_NOPE:], (1, 2, 0))),
        "wrsg": bf(jnp.concatenate([col(3, 4), col(8, 9), col(11, 12)], axis=1)),
        "glag": p["gla_norm_g"][l][None],
        "wbra": bf(p["w_br_a"][l]),
        "wbrb": bf(p["w_br_b"][l]),
        "wbrc": bf(p["w_br_c"][l]),
        "scw": scw,
        "wo": bf(p["w_o"][l]),
        "g2": p["norm2_g"][l][None],
        "wg": bf(p["w_ffn_gate"][l]),
        "wu": bf(p["w_ffn_up"][l]),
        "cw": cw,
        "wd": bf(p["w_ffn_down"][l]),
    }


def _pick(n, options):
    for o in options:
        if n % o == 0:
            return o
    raise ValueError(f"no tile in {options} divides {n}")


def kernel(x, c, ctx, c_ctx, w_ada, b_ada, norm1_g, w_in, w_gk2, b_gk2, gla_norm_g, mla_q_norm_g, w_uq,
           mla_kv_norm_g, w_ukv, sc_w, w_br_a, w_br_b, w_br_c, w_o, norm2_g, w_ffn_gate, w_ffn_up,
           ffn_conv_w, ffn_conv_b, w_ffn_down, final_norm_g):
    p = dict(w_in=w_in, w_gk2=w_gk2, b_gk2=b_gk2, gla_norm_g=gla_norm_g, mla_q_norm_g=mla_q_norm_g,
             w_uq=w_uq, mla_kv_norm_g=mla_kv_norm_g, w_ukv=w_ukv, sc_w=sc_w, w_br_a=w_br_a,
             w_br_b=w_br_b, w_br_c=w_br_c, w_o=w_o, norm1_g=norm1_g, norm2_g=norm2_g,
             w_ffn_gate=w_ffn_gate, w_ffn_up=w_ffn_up, ffn_conv_w=ffn_conv_w, ffn_conv_b=ffn_conv_b,
             w_ffn_down=w_ffn_down)
    b, l, d = x.shape
    lc = ctx.shape[1]
    depth = w_in.shape[0]
    assert l % PAIR == 0 and lc % PAIR == 0 and l % GRID_W == 0

    tiles = (512, 256, 128)
    tm, tm_c = _pick(l, tiles), _pick(lc, tiles)
    tb, tb_c = _pick(l, tiles), _pick(lc, tiles)
    tq, tq_c = _pick(l, (256, 128)), _pick(lc, (256, 128))
    key_tiles = (768, 512, 384, 256, 128)
    tk, tk_c = _pick(l + lc, key_tiles), _pick(lc, key_tiles)

    rows = -(-(b + 1) // 8) * 8
    cc = jnp.concatenate([c, c_ctx[None], jnp.zeros((rows - b - 1, d), F32)], axis=0)
    mod_all = _mod_call(cc, w_ada, b_ada)
    tabs = _rope_tables(l, True)
    tabs_c = _rope_tables(lc, False)
    gfinal = final_norm_g[None]
    zero_state = jnp.zeros((b, 2, V_W, QK_W), F32)

    xc = ctx
    for li in range(depth):
        last = li == depth - 1
        lw = _layer_weights(li, p)
        mod = mod_all[li, :b].reshape(b, 6, d)
        mod_c = jnp.broadcast_to(mod_all[li, b].reshape(1, 6, d), (b, 6, d))

        qk_c, v_c, vt_c, la_c, qt_c, kabs_c, ct_c, u_c = _proj_call(xc, mod_c, lw, tabs_c, tm_c)
        of_c, ob_c, s_ctx = _gla_call(qk_c, v_c, vt_c, la_c, zero_state, tb_c)

        qk, v, vt, la, qt, kabs, ct, u = _proj_call(x, mod, lw, tabs, tm)
        o_f, o_b, _ = _gla_call(qk, v, vt, la, s_ctx, tb)
        k_all = jnp.concatenate([kabs, kabs_c], axis=1)
        ct_all = jnp.concatenate([ct, ct_c], axis=2)
        o_att = _attn_call(qt, k_all, ct_all, lw["wuv"], tq, tk)
        x = _merge_call(x, mod, o_f, o_b, o_att, u, lw, tm)
        if not last:
            o_att_c = _attn_call(qt_c, kabs_c, ct_c, lw["wuv"], tq_c, tk_c)
            xc = _merge_call(xc, mod_c, of_c, ob_c, o_att_c, u_c, lw, tm_c)
        x = _ffn_call(x, mod, lw, gfinal, tm, last)
        if not last:
            xc = _ffn_call(xc, mod_c, lw, gfinal, tm_c, False)
    return x
```

```python
import functools

import numpy as np
import jax
import jax.numpy as jnp
from jax import lax
from jax.experimental import pallas as pl
from jax.experimental.pallas import tpu as pltpu

F32 = jnp.float32
BF = jnp.bfloat16

GRID_W = 64
GLA_HEADS = 4
GLA_DK = 64
GLA_DV = 128
GLA_RANK = 16
GLA_GATE_NORM = 16.0
GLA_CHUNK = 64
MLA_HEADS = 8
MLA_Q_LORA = 256
MLA_KV_LORA = 128
MLA_NOPE = 64
MLA_ROPE = 32
MLA_DV = 64
ROPE_THETA = 10000.0
ROPE_FREQS = MLA_ROPE // 4
SC_W = 512
NORM_EPS = 1e-6
N_BRANCH = 3

QK_W = GLA_HEADS * GLA_DK
V_W = GLA_HEADS * GLA_DV
Q_ABS = MLA_KV_LORA + MLA_ROPE
PAIR = 2 * GLA_CHUNK
HALO = 8
NEG_BIG = -1e30
ONES_ROWS = 16
LOG2E = 1.4426950408889634
KEY_SUB = 256

VMEM_LIMIT = 56 * 1024 * 1024


def _dot(a, b):
    return jnp.dot(a, b, preferred_element_type=F32)


def _dot_nt(a, b):
    return lax.dot_general(a, b, (((1,), (1,)), ((), ())), preferred_element_type=F32)


def _rms(x, g):
    return x * lax.rsqrt(jnp.mean(x * x, axis=-1, keepdims=True) + NORM_EPS) * g


def _silu(x):
    return x * jax.nn.sigmoid(x)


def _const_spec(shape):
    nd = len(shape)
    return pl.BlockSpec(shape, lambda *_: (0,) * nd, pipeline_mode=pl.Buffered(1))


def _params(sem):
    return pltpu.CompilerParams(dimension_semantics=sem, vmem_limit_bytes=VMEM_LIMIT)


def _mod_kernel(c_ref, w_ref, b_ref, o_ref):
    a = _silu(c_ref[...]).astype(BF)
    o_ref[0] = _dot(a, w_ref[0].astype(BF)) + b_ref[0]


def _mod_call(cc, w_ada, b_ada):
    nl, d, n = w_ada.shape
    rows = cc.shape[0]
    tn = 1024
    return pl.pallas_call(
        _mod_kernel,
        grid=(nl, n // tn),
        in_specs=[
            pl.BlockSpec((rows, d), lambda l, j: (0, 0)),
            pl.BlockSpec((1, d, tn), lambda l, j: (l, 0, j)),
            pl.BlockSpec((1, 1, tn), lambda l, j: (l, 0, j)),
        ],
        out_specs=pl.BlockSpec((1, rows, tn), lambda l, j: (l, 0, j)),
        out_shape=jax.ShapeDtypeStruct((nl, rows, n), F32),
        compiler_params=_params(("parallel", "parallel")),
        name="adaln_mod",
    )(cc, w_ada, b_ada.reshape(nl, 1, n))


def _proj_kernel(x_ref, mod_ref, g1_ref, wqkv_ref, wsm1_ref, wsm2_ref, wgk_ref, bgk_ref,
                 wcq_ref, gq_ref, wuqn_ref, wuk_ref, wuqr_ref, wuqrr_ref, wckv_ref, gkv_ref,
                 ws_ref, cosq_ref, sinq_ref, cosk_ref, sink_ref,
                 qk_ref, v_ref, vt_ref, la_ref, qt_ref, kabs_ref, ct_ref, u_ref):
    x = x_ref[0]
    m = mod_ref[0]
    h = _rms(x, g1_ref[...]) * (1.0 + m[1:2]) + m[0:1]
    hb = h.astype(BF)

    qkv = _dot(hb, wqkv_ref[...])
    qk_ref[0] = qkv[:, :2 * QK_W].astype(BF)
    v = qkv[:, 2 * QK_W:]
    v_ref[0] = v.astype(BF)
    vt_ref[0] = v.T.astype(BF)

    sm1 = _dot(hb, wsm1_ref[...])
    sm2 = _dot(hb, wsm2_ref[...])
    z = _dot(sm1.astype(BF), wgk_ref[...]) + bgk_ref[...]
    la_ref[0] = jax.nn.log_sigmoid(z) * (1.0 / GLA_GATE_NORM)

    scale = (MLA_NOPE + MLA_ROPE) ** -0.5 * LOG2E
    cqn = _rms(_dot(hb, wcq_ref[...]), gq_ref[...]).astype(BF)
    qnt = _dot_nt(wuqn_ref[...], cqn).astype(BF)
    qrope_t = (_dot_nt(wuqr_ref[...], cqn) * cosq_ref[...]
               + _dot_nt(wuqrr_ref[...], cqn) * sinq_ref[...]) * scale
    for hh in range(MLA_HEADS):
        qa = _dot(wuk_ref[hh], qnt[MLA_NOPE * hh:MLA_NOPE * (hh + 1), :]) * scale
        qt_ref[0, Q_ABS * hh:Q_ABS * hh + MLA_KV_LORA, :] = qa.astype(BF)
        qt_ref[0, Q_ABS * hh + MLA_KV_LORA:Q_ABS * (hh + 1), :] = (
            qrope_t[MLA_ROPE * hh:MLA_ROPE * (hh + 1), :].astype(BF))

    cn = _rms(_dot(hb, wckv_ref[...]), gkv_ref[...])
    kabs_ref[0, :, 0:MLA_KV_LORA] = cn.astype(BF)
    kr = sm1 * cosk_ref[...] + sm2 * sink_ref[...]
    kabs_ref[0, :, MLA_KV_LORA:Q_ABS] = kr[:, 0:MLA_ROPE].astype(BF)
    ct_ref[0] = cn.T.astype(BF)

    s = _dot(hb, ws_ref[...])
    u_ref[0] = s[:, :SC_W] * s[:, SC_W:]


def _proj_call(x, mod, lw, tabs, tm):
    b, l, d = x.shape
    cosq, sinq, cosk, sink = tabs
    tok = lambda c: pl.BlockSpec((1, tm, c), lambda bi, i: (bi, i, 0))
    tr = lambda r: pl.BlockSpec((1, r, tm), lambda bi, i: (bi, 0, i))
    weights = [lw["g1"], lw["wqkv"], lw["wsm1"], lw["wsm2"], lw["wgk"], lw["bgk"], lw["wcq"],
               lw["gq"], lw["wuqn"], lw["wuk"], lw["wuqr"], lw["wuqrr"], lw["wckv"], lw["gkv"],
               lw["ws"]]
    in_specs = ([tok(d), pl.BlockSpec((1, 6, d), lambda bi, i: (bi, 0, 0))]
                + [_const_spec(w.shape) for w in weights]
                + [pl.BlockSpec((MLA_HEADS * MLA_ROPE, tm), lambda bi, i: (0, i)),
                   pl.BlockSpec((MLA_HEADS * MLA_ROPE, tm), lambda bi, i: (0, i)),
                   pl.BlockSpec((tm, 128), lambda bi, i: (i, 0)),
                   pl.BlockSpec((tm, 128), lambda bi, i: (i, 0))])
    out_shape = (
        jax.ShapeDtypeStruct((b, l, 2 * QK_W), BF),
        jax.ShapeDtypeStruct((b, l, V_W), BF),
        jax.ShapeDtypeStruct((b, V_W, l), BF),
        jax.ShapeDtypeStruct((b, l, 2 * QK_W), F32),
        jax.ShapeDtypeStruct((b, MLA_HEADS * Q_ABS, l), BF),
        jax.ShapeDtypeStruct((b, l, Q_ABS), BF),
        jax.ShapeDtypeStruct((b, MLA_KV_LORA, l), BF),
        jax.ShapeDtypeStruct((b, l, SC_W), F32),
    )
    out_specs = (tok(2 * QK_W), tok(V_W), tr(V_W), tok(2 * QK_W), tr(MLA_HEADS * Q_ABS),
                 tok(Q_ABS), tr(MLA_KV_LORA), tok(SC_W))
    return pl.pallas_call(
        _proj_kernel,
        grid=(b, l // tm),
        in_specs=in_specs,
        out_specs=out_specs,
        out_shape=out_shape,
        compiler_params=_params(("parallel", "parallel")),
        name="proj",
    )(x, mod, *weights, cosq, sinq, cosk, sink)


def _gla_kernel(qkf_ref, qkb_ref, vf_ref, vb_ref, vtf_ref, vtb_ref, laf_ref, lab_ref, s0_ref,
                of_ref, ob_ref, sfin_ref, s_ref, *, tb):
    n = pl.program_id(1)
    npairs = tb // PAIR

    @pl.when(n == 0)
    def _():
        s_ref[...] = s0_ref[0]

    r = lax.broadcasted_iota(jnp.int32, (PAIR, PAIR), 0)
    c = lax.broadcasted_iota(jnp.int32, (PAIR, PAIR), 1)
    same = (r // GLA_CHUNK) == (c // GLA_CHUNK)
    ones_c = jnp.where(same, 1.0, 0.0).astype(BF)
    lane_head = lax.broadcasted_iota(jnp.int32, (PAIR, QK_W), 1) // GLA_DK
    row_chunk = lax.broadcasted_iota(jnp.int32, (PAIR, QK_W), 0) // GLA_CHUNK
    bd_mask = (lax.broadcasted_iota(jnp.int32, (V_W, QK_W), 0) // GLA_DV
               == lax.broadcasted_iota(jnp.int32, (V_W, QK_W), 1) // GLA_DK)

    def pair_dir(qk_ref, v_ref, vt_ref, la_ref, o_ref, sidx, p, forward):
        start = pl.multiple_of(p * PAIR, PAIR)
        rows = pl.ds(start, PAIR)
        tri = jnp.logical_and(same, (c <= r) if forward else (c >= r))
        tri_b = jnp.where(tri, 1.0, 0.0).astype(BF)
        la = la_ref[0, rows, :]
        hi = la.astype(BF)
        lo = (la - hi.astype(F32)).astype(BF)
        bcum = _dot(tri_b, hi) + _dot(tri_b, lo)
        blast = _dot(ones_c, hi) + _dot(ones_c, lo)
        qk = qk_ref[0, rows, :].astype(F32)
        q = qk[:, :QK_W] * (GLA_DK ** -0.5)
        k = qk[:, QK_W:]
        q_in = (q * jnp.exp(bcum)).astype(BF)
        k_in = (k * jnp.exp(-bcum)).astype(BF)
        k_st = (k * jnp.exp(blast - bcum)).astype(BF)
        decay = jnp.exp(blast)
        qs = jnp.concatenate(
            [jnp.where(lane_head == hh, q_in, jnp.zeros_like(q_in)) for hh in range(GLA_HEADS)], axis=0)
        a = _dot_nt(qs, k_in)
        tri4 = jnp.concatenate([tri] * GLA_HEADS, axis=0)
        a = jnp.where(tri4, a, 0.0).astype(BF)
        v = v_ref[0, rows, :]
        o_intra = jnp.concatenate(
            [_dot(a[PAIR * hh:PAIR * (hh + 1)], v[:, GLA_DV * hh:GLA_DV * (hh + 1)])
             for hh in range(GLA_HEADS)], axis=1)
        vt = vt_ref[0, :, rows]
        for ci in ((0, 1) if forward else (1, 0)):
            lo_r, hi_r = GLA_CHUNK * ci, GLA_CHUNK * (ci + 1)
            st = s_ref[sidx]
            o_inter = _dot_nt(q_in[lo_r:hi_r], st.astype(BF))
            o_ref[0, pl.ds(start + lo_r, GLA_CHUNK), :] = o_intra[lo_r:hi_r] + o_inter
            k_c = jnp.where(row_chunk == ci, k_st, jnp.zeros_like(k_st))
            ut = _dot(vt, k_c)
            s_ref[sidx] = st * decay[lo_r:lo_r + 1, :] + jnp.where(bd_mask, ut, 0.0)

    def body(p, carry):
        pair_dir(qkf_ref, vf_ref, vtf_ref, laf_ref, of_ref, 0, p, True)
        pair_dir(qkb_ref, vb_ref, vtb_ref, lab_ref, ob_ref, 1, npairs - 1 - p, False)
        return carry

    lax.fori_loop(0, npairs, body, 0)

    @pl.when(n == pl.num_programs(1) - 1)
    def _():
        sfin_ref[0] = s_ref[...]


def _gla_call(qk, v, vt, la, s0, tb):
    b, l, _ = qk.shape
    nb = l // tb
    fwd = lambda c: pl.BlockSpec((1, tb, c), lambda bi, i: (bi, i, 0))
    bwd = lambda c: pl.BlockSpec((1, tb, c), lambda bi, i: (bi, nb - 1 - i, 0))
    st_spec = pl.BlockSpec((1, 2, V_W, QK_W), lambda bi, i: (bi, 0, 0, 0))
    return pl.pallas_call(
        functools.partial(_gla_kernel, tb=tb),
        grid=(b, nb),
        in_specs=[fwd(2 * QK_W), bwd(2 * QK_W), fwd(V_W), bwd(V_W),
                  pl.BlockSpec((1, V_W, tb), lambda bi, i: (bi, 0, i)),
                  pl.BlockSpec((1, V_W, tb), lambda bi, i: (bi, 0, nb - 1 - i)),
                  pl.BlockSpec((1, tb, QK_W), lambda bi, i: (bi, i, 0)),
                  pl.BlockSpec((1, tb, QK_W), lambda bi, i: (bi, nb - 1 - i, 1)),
                  st_spec],
        out_specs=(fwd(V_W), bwd(V_W), st_spec),
        out_shape=(jax.ShapeDtypeStruct((b, l, V_W), F32),
                   jax.ShapeDtypeStruct((b, l, V_W), F32),
                   jax.ShapeDtypeStruct((b, 2, V_W, QK_W), F32)),
        scratch_shapes=[pltpu.VMEM((2, V_W, QK_W), F32)],
        compiler_params=_params(("parallel", "arbitrary")),
        name="gla",
    )(qk, qk, v, v, vt, vt, la, la, s0)


def _attn_kernel(qt_ref, k_ref, ct_ref, wuv_ref, o_ref, m_ref, acc_ref, ot_ref, sa_ref, sb_ref, pa_ref, pb_ref,
                 *, tk):
    tq = qt_ref.shape[2]
    nchunks = k_ref.shape[1] // tk
    nsub = tk // KEY_SUB
    sbufs = (sa_ref, sb_ref)
    pbufs = (pa_ref, pb_ref)
    m_ref[...] = jnp.full(m_ref.shape, NEG_BIG, F32)
    acc_ref[...] = jnp.zeros(acc_ref.shape, F32)

    def score_sub(kstart, hh, i, sbuf, mpart):
        ks = k_ref[0, pl.ds(kstart + KEY_SUB * i, KEY_SUB), :]
        s = _dot(ks, qt_ref[0, Q_ABS * hh:Q_ABS * (hh + 1), :])
        sbuf[KEY_SUB * i:KEY_SUB * (i + 1), :] = s
        return jnp.maximum(mpart, jnp.max(s.reshape(KEY_SUB // 8, 8, tq), axis=0))

    def score_finish(hh, mpart):
        m_old = m_ref[hh:hh + 1, :]
        m_new = jnp.maximum(m_old, jnp.max(mpart, axis=0, keepdims=True))
        m_ref[hh:hh + 1, :] = m_new
        return m_new, jnp.exp2(m_old - m_new)

    def value_sub(i, sbuf, pbuf, m_new):
        rows = slice(KEY_SUB * i, KEY_SUB * (i + 1))
        pbuf[rows, :] = jnp.exp2(sbuf[rows, :] - m_new).astype(BF)

    mpart0 = jnp.full((8, tq), NEG_BIG, F32)
    mp = mpart0
    for i in range(nsub):
        mp = score_sub(0, 0, i, sbufs[0], mp)
    first = score_finish(0, mp)

    def chunk_body(j, carry):
        m_cur, alpha_cur = carry
        start = pl.multiple_of(j * tk, KEY_SUB)
        start_next = pl.multiple_of(jnp.minimum(j + 1, nchunks - 1) * tk, KEY_SUB)
        for hh in range(MLA_HEADS):
            nh = (hh + 1) % MLA_HEADS
            nstart = start if hh + 1 < MLA_HEADS else start_next
            mp = mpart0
            for i in range(nsub):
                value_sub(i, sbufs[hh % 2], pbufs[hh % 2], m_cur)
                mp = score_sub(nstart, nh, i, sbufs[1 - hh % 2], mp)
            cc = ct_ref[0, :, pl.ds(start, tk)]
            acc_ref[hh] = alpha_cur * acc_ref[hh] + _dot(cc, pbufs[hh % 2][...])
            m_cur, alpha_cur = score_finish(nh, mp)
        return m_cur, alpha_cur

    lax.fori_loop(0, nchunks, chunk_body, first)
    for hh in range(MLA_HEADS):
        a = acc_ref[hh]
        o_lat = (a[:MLA_KV_LORA] / a[MLA_KV_LORA:MLA_KV_LORA + 1]).astype(BF)
        ot_ref[MLA_DV * hh:MLA_DV * (hh + 1), :] = _dot(wuv_ref[hh], o_lat)
    o_ref[0] = ot_ref[...].T.astype(BF)


def _attn_call(qt, kabs, ct, wuv, tq, tk):
    b, _, l = qt.shape
    lk = kabs.shape[1]
    ct1 = jnp.concatenate([ct, jnp.ones((b, ONES_ROWS, lk), BF)], axis=1)
    vrows = MLA_KV_LORA + ONES_ROWS
    return pl.pallas_call(
        functools.partial(_attn_kernel, tk=tk),
        grid=(b, l // tq),
        in_specs=[pl.BlockSpec((1, MLA_HEADS * Q_ABS, tq), lambda bi, i: (bi, 0, i)),
                  pl.BlockSpec((1, lk, Q_ABS), lambda bi, i: (bi, 0, 0)),
                  pl.BlockSpec((1, vrows, lk), lambda bi, i: (bi, 0, 0)),
                  _const_spec(wuv.shape)],
        out_specs=pl.BlockSpec((1, tq, MLA_HEADS * MLA_DV), lambda bi, i: (bi, i, 0)),
        out_shape=jax.ShapeDtypeStruct((b, l, MLA_HEADS * MLA_DV), BF),
        scratch_shapes=[pltpu.VMEM((MLA_HEADS, tq), F32),
                        pltpu.VMEM((MLA_HEADS, vrows, tq), F32),
                        pltpu.VMEM((MLA_HEADS * MLA_DV, tq), F32),
                        pltpu.VMEM((tk, tq), F32),
                        pltpu.VMEM((tk, tq), F32),
                        pltpu.VMEM((tk, tq), BF),
                        pltpu.VMEM((tk, tq), BF)],
        compiler_params=_params(("parallel", "parallel")),
        name="attn",
    )(qt, kabs, ct1, wuv)


def _shift_rows(t, prev_row, next_row):
    n = t.shape[0]
    row = lax.broadcasted_iota(jnp.int32, t.shape, 0)
    down = jnp.where(row == 0, prev_row, pltpu.roll(t, 1, 0))
    up = jnp.where(row == n - 1, next_row, pltpu.roll(t, n - 1, 0))
    return down, up


def _merge_kernel(x_ref, mod_ref, g1_ref, of_ref, ob_ref, oat_ref, u_ref, up_ref, un_ref,
                  wrsg_ref, glag_ref, wbra_ref, wbrb_ref, wbrc_ref, scw_ref, wo_ref, xo_ref):
    i = pl.program_id(1)
    x = x_ref[0]
    m = mod_ref[0]
    d = x.shape[1]
    hb = (_rms(x, g1_ref[...]) * (1.0 + m[1:2]) + m[0:1]).astype(BF)
    rsg = _dot(hb, wrsg_ref[...])
    r_a = rsg[:, :V_W]
    sb = rsg[:, V_W:V_W + SC_W]
    gates = jax.nn.sigmoid(rsg[:, V_W + SC_W:])

    o = of_ref[0] + ob_ref[0]
    gg = glag_ref[...]
    on = jnp.concatenate(
        [_rms(o[:, GLA_DV * hh:GLA_DV * (hh + 1)], gg[:, GLA_DV * hh:GLA_DV * (hh + 1)])
         for hh in range(GLA_HEADS)], axis=1)
    y_a = _dot((on * _silu(r_a)).astype(BF), wbra_ref[...])
    y_b = _dot(oat_ref[0], wbrb_ref[...])

    u = u_ref[0]
    prev_row = jnp.where(i > 0, up_ref[0, HALO - 1:HALO, :], 0.0)
    next_row = jnp.where(i < pl.num_programs(1) - 1, un_ref[0, 0:1, :], 0.0)
    u_dn, u_up = _shift_rows(u, prev_row, next_row)
    w = scw_ref[...]
    conv = u_dn * w[0:1] + u * w[1:2] + u_up * w[2:3]
    y_c = _dot((sb * conv).astype(BF), wbrc_ref[...])

    mix = gates[:, :d] * y_a + gates[:, d:2 * d] * y_b + gates[:, 2 * d:] * y_c
    xo_ref[0] = x + m[2:3] * _dot(mix.astype(BF), wo_ref[...])


def _halo_specs(tm, l, c):
    nb8 = tm // HALO
    last8 = l // HALO - 1
    prev = pl.BlockSpec((1, HALO, c), lambda bi, i: (bi, jnp.maximum(i * nb8 - 1, 0), 0))
    nxt = pl.BlockSpec((1, HALO, c), lambda bi, i: (bi, jnp.minimum((i + 1) * nb8, last8), 0))
    return prev, nxt


def _merge_call(x, mod, o_f, o_b, o_att, u, lw, tm):
    b, l, d = x.shape
    tok = lambda c: pl.BlockSpec((1, tm, c), lambda bi, i: (bi, i, 0))
    up_spec, un_spec = _halo_specs(tm, l, SC_W)
    weights = [lw["wrsg"], lw["glag"], lw["wbra"], lw["wbrb"], lw["wbrc"], lw["scw"], lw["wo"]]
    return pl.pallas_call(
        _merge_kernel,
        grid=(b, l // tm),
        in_specs=[tok(d), pl.BlockSpec((1, 6, d), lambda bi, i: (bi, 0, 0)), _const_spec(lw["g1"].shape),
                  tok(V_W), tok(V_W), tok(MLA_HEADS * MLA_DV), tok(SC_W), up_spec, un_spec]
                 + [_const_spec(w.shape) for w in weights],
        out_specs=tok(d),
        out_shape=jax.ShapeDtypeStruct((b, l, d), F32),
        compiler_params=_params(("parallel", "parallel")),
        name="merge",
    )(x, mod, lw["g1"], o_f, o_b, o_att, u, u, u, *weights)


def _ffn_kernel(x_ref, xp_ref, xn_ref, mod_ref, g2_ref, wg_ref, wu_ref, cw_ref, wd_ref, gf_ref,
                xo_ref, *, fc, final):
    i = pl.program_id(1)
    x = x_ref[0]
    m = mod_ref[0]
    tm = x.shape[0]
    nf = wg_ref.shape[1]

    def hmod(t):
        return _rms(t, g2_ref[...]) * (1.0 + m[4:5]) + m[3:4]

    h = hmod(x)
    hb = h.astype(BF)
    h_ext = jnp.concatenate([hmod(xp_ref[0]), h, hmod(xn_ref[0])], axis=0).astype(BF)
    has_prev = i > 0
    has_next = i < pl.num_programs(1) - 1
    row = lax.broadcasted_iota(jnp.int32, (tm, fc), 0)

    acc = jnp.zeros((tm, x.shape[1]), F32)
    for ci in range(nf // fc):
        cols = slice(ci * fc, (ci + 1) * fc)
        g_ext = _dot(h_ext, wg_ref[:, cols])
        g_mid = g_ext[HALO:HALO + tm]
        g_dn = g_ext[HALO - 1:HALO - 1 + tm]
        g_up = g_ext[HALO + 1:HALO + 1 + tm]
        g_dn = jnp.where(jnp.logical_and(row == 0, jnp.logical_not(has_prev)), 0.0, g_dn)
        g_up = jnp.where(jnp.logical_and(row == tm - 1, jnp.logical_not(has_next)), 0.0, g_up)
        cw = cw_ref[:, cols]
        g = g_dn * cw[0:1] + g_mid * cw[1:2] + g_up * cw[2:3] + cw[3:4]
        a = (_silu(g) * _dot(hb, wu_ref[:, cols])).astype(BF)
        acc = acc + _dot(a, wd_ref[cols, :])
    y = x + m[5:6] * acc
    if final:
        y = _rms(y, gf_ref[...])
    xo_ref[0] = y


def _ffn_call(x, mod, lw, gfinal, tm, final):
    b, l, d = x.shape
    nf = lw["wg"].shape[1]
    fc = 256
    tok = pl.BlockSpec((1, tm, d), lambda bi, i: (bi, i, 0))
    xp_spec, xn_spec = _halo_specs(tm, l, d)
    weights = [lw["g2"], lw["wg"], lw["wu"], lw["cw"], lw["wd"], gfinal]
    return pl.pallas_call(
        functools.partial(_ffn_kernel, fc=fc, final=final),
        grid=(b, l // tm),
        in_specs=[tok, xp_spec, xn_spec, pl.BlockSpec((1, 6, d), lambda bi, i: (bi, 0, 0))]
                 + [_const_spec(w.shape) for w in weights],
        out_specs=tok,
        out_shape=jax.ShapeDtypeStruct((b, l, d), F32),
        compiler_params=_params(("parallel", "parallel")),
        name="ffn",
    )(x, x, x, mod, *weights)


def _rope_perm():
    idx = np.zeros((MLA_ROPE,), np.int32)
    sgn = np.zeros((MLA_ROPE,), np.float32)
    for ax in range(2):
        for f in range(ROPE_FREQS):
            lo = ax * 2 * ROPE_FREQS + f
            hi = lo + ROPE_FREQS
            idx[lo], sgn[lo] = hi, -1.0
            idx[hi], sgn[hi] = lo, 1.0
    return idx, sgn


def _rope_tables(l, rotate):
    if rotate:
        rows = l // GRID_W
        row = jnp.repeat(jnp.arange(rows, dtype=F32), GRID_W)
        col = jnp.tile(jnp.arange(GRID_W, dtype=F32), rows)
        inv = ROPE_THETA ** (-jnp.arange(ROPE_FREQS, dtype=F32) / ROPE_FREQS)
        ang = jnp.stack([row[:, None] * inv, col[:, None] * inv], axis=1)
        cos, sin = jnp.cos(ang), jnp.sin(ang)
        expand = lambda t: jnp.broadcast_to(t[:, :, None, :], (l, 2, 2, ROPE_FREQS)).reshape(l, MLA_ROPE)
        cos32, sin32 = expand(cos), expand(sin)
    else:
        cos32, sin32 = jnp.ones((l, MLA_ROPE), F32), jnp.zeros((l, MLA_ROPE), F32)
    cosq = jnp.tile(cos32, (1, MLA_HEADS)).T
    sinq = jnp.tile(sin32, (1, MLA_HEADS)).T
    pad = lambda t: jnp.pad(t, ((0, 0), (0, 128 - MLA_ROPE)))
    return cosq, sinq, pad(cos32), pad(sin32)


def _layer_weights(l, p):
    d = p["w_in"].shape[1]
    w_in = p["w_in"][l]
    sizes = (QK_W, QK_W, V_W, V_W, 2 * GLA_RANK, MLA_Q_LORA, MLA_KV_LORA, MLA_ROPE, SC_W, SC_W, SC_W,
             N_BRANCH * d)
    offs = np.concatenate([[0], np.cumsum(sizes)])
    col = lambda a, b: w_in[:, int(offs[a]):int(offs[b])]
    idx, sgn = _rope_perm()
    w_kr = col(7, 8)
    w_kr_rot = w_kr[:, idx] * sgn
    zeros = lambda n: jnp.zeros((d, n), F32)
    wgk = jnp.zeros((128, 2 * QK_W), F32)
    wgk = wgk.at[MLA_ROPE:MLA_ROPE + GLA_RANK, :QK_W].set(p["w_gk2"][l, 0])
    wgk = wgk.at[MLA_ROPE + GLA_RANK:MLA_ROPE + 2 * GLA_RANK, QK_W:].set(p["w_gk2"][l, 1])
    w_uq = p["w_uq"][l].reshape(MLA_Q_LORA, MLA_HEADS, MLA_NOPE + MLA_ROPE)
    w_uq_rope = w_uq[:, :, MLA_NOPE:]
    w_ukv = p["w_ukv"][l].reshape(MLA_KV_LORA, MLA_HEADS, MLA_NOPE + MLA_DV)
    nf = p["w_ffn_gate"].shape[2]
    cw = jnp.concatenate([p["ffn_conv_w"][l], p["ffn_conv_b"][l][None], jnp.zeros((4, nf), F32)], axis=0)
    scw = jnp.concatenate([p["sc_w"][l], jnp.zeros((5, SC_W), F32)], axis=0)
    bf = lambda t: t.astype(BF)
    return {
        "g1": p["norm1_g"][l][None],
        "wqkv": bf(col(0, 3)),
        "wsm1": bf(jnp.concatenate([w_kr, col(4, 5), zeros(128 - MLA_ROPE - 2 * GLA_RANK)], axis=1)),
        "wsm2": bf(jnp.concatenate([w_kr_rot, zeros(128 - MLA_ROPE)], axis=1)),
        "wgk": bf(wgk),
        "bgk": p["b_gk2"][l].reshape(1, 2 * QK_W),
        "wcq": bf(col(5, 6)),
        "gq": p["mla_q_norm_g"][l][None],
        "wuqn": bf(w_uq[:, :, :MLA_NOPE].reshape(MLA_Q_LORA, -1).T),
        "wuk": bf(jnp.transpose(w_ukv[:, :, :MLA_NOPE], (1, 0, 2))),
        "wuqr": bf(w_uq_rope.reshape(MLA_Q_LORA, -1).T),
        "wuqrr": bf((w_uq_rope[:, :, idx] * sgn).reshape(MLA_Q_LORA, -1).T),
        "wckv": bf(col(6, 7)),
        "gkv": p["mla_kv_norm_g"][l][None],
        "ws": bf(col(9, 11)),
        "wuv": bf(jnp.transpose(w_ukv[:, :, MLA_NOPE:], (1, 2, 0))),
        "wrsg": bf(jnp.concatenate([col(3, 4), col(8, 9), col(11, 12)], axis=1)),
        "glag": p["gla_norm_g"][l][None],
        "wbra": bf(p["w_br_a"][l]),
        "wbrb": bf(p["w_br_b"][l]),
        "wbrc": bf(p["w_br_c"][l]),
        "scw": scw,
        "wo": bf(p["w_o"][l]),
        "g2": p["norm2_g"][l][None],
        "wg": bf(p["w_ffn_gate"][l]),
        "wu": bf(p["w_ffn_up"][l]),
        "cw": cw,
        "wd": bf(p["w_ffn_down"][l]),
    }


def _pick(n, options):
    for o in options:
        if n % o == 0:
            return o
    raise ValueError(f"no tile in {options} divides {n}")


def kernel(x, c, ctx, c_ctx, w_ada, b_ada, norm1_g, w_in, w_gk2, b_gk2, gla_norm_g, mla_q_norm_g, w_uq,
           mla_kv_norm_g, w_ukv, sc_w, w_br_a, w_br_b, w_br_c, w_o, norm2_g, w_ffn_gate, w_ffn_up,
           ffn_conv_w, ffn_conv_b, w_ffn_down, final_norm_g):
    p = dict(w_in=w_in, w_gk2=w_gk2, b_gk2=b_gk2, gla_norm_g=gla_norm_g, mla_q_norm_g=mla_q_norm_g,
             w_uq=w_uq, mla_kv_norm_g=mla_kv_norm_g, w_ukv=w_ukv, sc_w=sc_w, w_br_a=w_br_a,
             w_br_b=w_br_b, w_br_c=w_br_c, w_o=w_o, norm1_g=norm1_g, norm2_g=norm2_g,
             w_ffn_gate=w_ffn_gate, w_ffn_up=w_ffn_up, ffn_conv_w=ffn_conv_w, ffn_conv_b=ffn_conv_b,
             w_ffn_down=w_ffn_down)
    b, l, d = x.shape
    lc = ctx.shape[1]
    depth = w_in.shape[0]
    assert l % PAIR == 0 and lc % PAIR == 0 and l % GRID_W == 0

    tiles = (512, 256, 128)
    tm, tm_c = _pick(l, tiles), _pick(lc, tiles)
    tb, tb_c = _pick(l, tiles), _pick(lc, tiles)
    tq, tq_c = _pick(l, (256, 128)), _pick(lc, (256, 128))
    key_tiles = (3 * KEY_SUB, 2 * KEY_SUB, KEY_SUB)
    tk, tk_c = _pick(l + lc, key_tiles), _pick(lc, key_tiles)

    rows = -(-(b + 1) // 8) * 8
    cc = jnp.concatenate([c, c_ctx[None], jnp.zeros((rows - b - 1, d), F32)], axis=0)
    mod_all = _mod_call(cc, w_ada, b_ada)
    tabs = _rope_tables(l, True)
    tabs_c = _rope_tables(lc, False)
    gfinal = final_norm_g[None]
    zero_state = jnp.zeros((b, 2, V_W, QK_W), F32)

    xc = ctx
    for li in range(depth):
        last = li == depth - 1
        lw = _layer_weights(li, p)
        mod = mod_all[li, :b].reshape(b, 6, d)
        mod_c = jnp.broadcast_to(mod_all[li, b].reshape(1, 6, d), (b, 6, d))

        qk_c, v_c, vt_c, la_c, qt_c, kabs_c, ct_c, u_c = _proj_call(xc, mod_c, lw, tabs_c, tm_c)
        of_c, ob_c, s_ctx = _gla_call(qk_c, v_c, vt_c, la_c, zero_state, tb_c)

        qk, v, vt, la, qt, kabs, ct, u = _proj_call(x, mod, lw, tabs, tm)
        o_f, o_b, _ = _gla_call(qk, v, vt, la, s_ctx, tb)
        k_all = jnp.concatenate([kabs, kabs_c], axis=1)
        ct_all = jnp.concatenate([ct, ct_c], axis=2)
        o_att = _attn_call(qt, k_all, ct_all, lw["wuv"], tq, tk)
        x = _merge_call(x, mod, o_f, o_b, o_att, u, lw, tm)
        if not last:
            o_att_c = _attn_call(qt_c, kabs_c, ct_c, lw["wuv"], tq_c, tk_c)
            xc = _merge_call(xc, mod_c, of_c, ob_c, o_att_c, u_c, lw, tm_c)
        x = _ffn_call(x, mod, lw, gfinal, tm, last)
        if not last:
            xc = _ffn_call(xc, mod_c, lw, gfinal, tm_c, False)
    return x
```

```python
import functools

import numpy as np
import jax
import jax.numpy as jnp
from jax import lax
from jax.experimental import pallas as pl
from jax.experimental.pallas import tpu as pltpu

F32 = jnp.float32
BF = jnp.bfloat16

GRID_W = 64
GLA_HEADS = 4
GLA_DK = 64
GLA_DV = 128
GLA_RANK = 16
GLA_GATE_NORM = 16.0
GLA_CHUNK = 64
MLA_HEADS = 8
MLA_Q_LORA = 256
MLA_KV_LORA = 128
MLA_NOPE = 64
MLA_ROPE = 32
MLA_DV = 64
ROPE_THETA = 10000.0
ROPE_FREQS = MLA_ROPE // 4
SC_W = 512
NORM_EPS = 1e-6
N_BRANCH = 3

QK_W = GLA_HEADS * GLA_DK
V_W = GLA_HEADS * GLA_DV
Q_ABS = MLA_KV_LORA + MLA_ROPE
PAIR = 2 * GLA_CHUNK
HALO = 8
NEG_BIG = -1e30
ONES_ROWS = 16
LOG2E = 1.4426950408889634
KEY_SUB = 256
ATTN_SKEW = 2
SCORE_SLOTS = 4
assert MLA_HEADS % SCORE_SLOTS == 0 and SCORE_SLOTS > ATTN_SKEW

VMEM_LIMIT = 56 * 1024 * 1024


def _dot(a, b):
    return jnp.dot(a, b, preferred_element_type=F32)


def _dot_nt(a, b):
    return lax.dot_general(a, b, (((1,), (1,)), ((), ())), preferred_element_type=F32)


def _rms(x, g):
    return x * lax.rsqrt(jnp.mean(x * x, axis=-1, keepdims=True) + NORM_EPS) * g


def _silu(x):
    return x * jax.nn.sigmoid(x)


def _const_spec(shape):
    nd = len(shape)
    return pl.BlockSpec(shape, lambda *_: (0,) * nd, pipeline_mode=pl.Buffered(1))


def _params(sem):
    return pltpu.CompilerParams(dimension_semantics=sem, vmem_limit_bytes=VMEM_LIMIT)


def _mod_kernel(c_ref, w_ref, b_ref, o_ref):
    a = _silu(c_ref[...]).astype(BF)
    o_ref[0] = _dot(a, w_ref[0].astype(BF)) + b_ref[0]


def _mod_call(cc, w_ada, b_ada):
    nl, d, n = w_ada.shape
    rows = cc.shape[0]
    tn = 1024
    return pl.pallas_call(
        _mod_kernel,
        grid=(nl, n // tn),
        in_specs=[
            pl.BlockSpec((rows, d), lambda l, j: (0, 0)),
            pl.BlockSpec((1, d, tn), lambda l, j: (l, 0, j)),
            pl.BlockSpec((1, 1, tn), lambda l, j: (l, 0, j)),
        ],
        out_specs=pl.BlockSpec((1, rows, tn), lambda l, j: (l, 0, j)),
        out_shape=jax.ShapeDtypeStruct((nl, rows, n), F32),
        compiler_params=_params(("parallel", "parallel")),
        name="adaln_mod",
    )(cc, w_ada, b_ada.reshape(nl, 1, n))


def _proj_kernel(x_ref, mod_ref, g1_ref, wqkv_ref, wsm1_ref, wsm2_ref, wgk_ref, bgk_ref,
                 wcq_ref, gq_ref, wuqn_ref, wuk_ref, wuqr_ref, wuqrr_ref, wckv_ref, gkv_ref,
                 ws_ref, cosq_ref, sinq_ref, cosk_ref, sink_ref,
                 qk_ref, v_ref, vt_ref, la_ref, qt_ref, kabs_ref, ct_ref, u_ref):
    x = x_ref[0]
    m = mod_ref[0]
    h = _rms(x, g1_ref[...]) * (1.0 + m[1:2]) + m[0:1]
    hb = h.astype(BF)

    qkv = _dot(hb, wqkv_ref[...])
    qk_ref[0] = qkv[:, :2 * QK_W].astype(BF)
    v = qkv[:, 2 * QK_W:]
    v_ref[0] = v.astype(BF)
    vt_ref[0] = v.T.astype(BF)

    sm1 = _dot(hb, wsm1_ref[...])
    sm2 = _dot(hb, wsm2_ref[...])
    z = _dot(sm1.astype(BF), wgk_ref[...]) + bgk_ref[...]
    la_ref[0] = jax.nn.log_sigmoid(z) * (1.0 / GLA_GATE_NORM)

    scale = (MLA_NOPE + MLA_ROPE) ** -0.5 * LOG2E
    cqn = _rms(_dot(hb, wcq_ref[...]), gq_ref[...]).astype(BF)
    qnt = _dot_nt(wuqn_ref[...], cqn).astype(BF)
    qrope_t = (_dot_nt(wuqr_ref[...], cqn) * cosq_ref[...]
               + _dot_nt(wuqrr_ref[...], cqn) * sinq_ref[...]) * scale
    for hh in range(MLA_HEADS):
        qa = _dot(wuk_ref[hh], qnt[MLA_NOPE * hh:MLA_NOPE * (hh + 1), :]) * scale
        qt_ref[0, Q_ABS * hh:Q_ABS * hh + MLA_KV_LORA, :] = qa.astype(BF)
        qt_ref[0, Q_ABS * hh + MLA_KV_LORA:Q_ABS * (hh + 1), :] = (
            qrope_t[MLA_ROPE * hh:MLA_ROPE * (hh + 1), :].astype(BF))

    cn = _rms(_dot(hb, wckv_ref[...]), gkv_ref[...])
    kabs_ref[0, :, 0:MLA_KV_LORA] = cn.astype(BF)
    kr = sm1 * cosk_ref[...] + sm2 * sink_ref[...]
    kabs_ref[0, :, MLA_KV_LORA:Q_ABS] = kr[:, 0:MLA_ROPE].astype(BF)
    ct_ref[0] = cn.T.astype(BF)

    s = _dot(hb, ws_ref[...])
    u_ref[0] = s[:, :SC_W] * s[:, SC_W:]


def _proj_call(x, mod, lw, tabs, tm):
    b, l, d = x.shape
    cosq, sinq, cosk, sink = tabs
    tok = lambda c: pl.BlockSpec((1, tm, c), lambda bi, i: (bi, i, 0))
    tr = lambda r: pl.BlockSpec((1, r, tm), lambda bi, i: (bi, 0, i))
    weights = [lw["g1"], lw["wqkv"], lw["wsm1"], lw["wsm2"], lw["wgk"], lw["bgk"], lw["wcq"],
               lw["gq"], lw["wuqn"], lw["wuk"], lw["wuqr"], lw["wuqrr"], lw["wckv"], lw["gkv"],
               lw["ws"]]
    in_specs = ([tok(d), pl.BlockSpec((1, 6, d), lambda bi, i: (bi, 0, 0))]
                + [_const_spec(w.shape) for w in weights]
                + [pl.BlockSpec((MLA_HEADS * MLA_ROPE, tm), lambda bi, i: (0, i)),
                   pl.BlockSpec((MLA_HEADS * MLA_ROPE, tm), lambda bi, i: (0, i)),
                   pl.BlockSpec((tm, 128), lambda bi, i: (i, 0)),
                   pl.BlockSpec((tm, 128), lambda bi, i: (i, 0))])
    out_shape = (
        jax.ShapeDtypeStruct((b, l, 2 * QK_W), BF),
        jax.ShapeDtypeStruct((b, l, V_W), BF),
        jax.ShapeDtypeStruct((b, V_W, l), BF),
        jax.ShapeDtypeStruct((b, l, 2 * QK_W), F32),
        jax.ShapeDtypeStruct((b, MLA_HEADS * Q_ABS, l), BF),
        jax.ShapeDtypeStruct((b, l, Q_ABS), BF),
        jax.ShapeDtypeStruct((b, MLA_KV_LORA, l), BF),
        jax.ShapeDtypeStruct((b, l, SC_W), F32),
    )
    out_specs = (tok(2 * QK_W), tok(V_W), tr(V_W), tok(2 * QK_W), tr(MLA_HEADS * Q_ABS),
                 tok(Q_ABS), tr(MLA_KV_LORA), tok(SC_W))
    return pl.pallas_call(
        _proj_kernel,
        grid=(b, l // tm),
        in_specs=in_specs,
        out_specs=out_specs,
        out_shape=out_shape,
        compiler_params=_params(("parallel", "parallel")),
        name="proj",
    )(x, mod, *weights, cosq, sinq, cosk, sink)


def _gla_kernel(qkf_ref, qkb_ref, vf_ref, vb_ref, vtf_ref, vtb_ref, laf_ref, lab_ref, s0_ref,
                of_ref, ob_ref, sfin_ref, s_ref, *, tb):
    n = pl.program_id(1)
    npairs = tb // PAIR

    @pl.when(n == 0)
    def _():
        s_ref[...] = s0_ref[0]

    r = lax.broadcasted_iota(jnp.int32, (PAIR, PAIR), 0)
    c = lax.broadcasted_iota(jnp.int32, (PAIR, PAIR), 1)
    same = (r // GLA_CHUNK) == (c // GLA_CHUNK)
    ones_c = jnp.where(same, 1.0, 0.0).astype(BF)
    lane_head = lax.broadcasted_iota(jnp.int32, (PAIR, QK_W), 1) // GLA_DK
    row_chunk = lax.broadcasted_iota(jnp.int32, (PAIR, QK_W), 0) // GLA_CHUNK
    bd_mask = (lax.broadcasted_iota(jnp.int32, (V_W, QK_W), 0) // GLA_DV
               == lax.broadcasted_iota(jnp.int32, (V_W, QK_W), 1) // GLA_DK)

    def pair_dir(qk_ref, v_ref, vt_ref, la_ref, o_ref, sidx, p, forward):
        start = pl.multiple_of(p * PAIR, PAIR)
        rows = pl.ds(start, PAIR)
        tri = jnp.logical_and(same, (c <= r) if forward else (c >= r))
        tri_b = jnp.where(tri, 1.0, 0.0).astype(BF)
        la = la_ref[0, rows, :]
        hi = la.astype(BF)
        lo = (la - hi.astype(F32)).astype(BF)
        bcum = _dot(tri_b, hi) + _dot(tri_b, lo)
        blast = _dot(ones_c, hi) + _dot(ones_c, lo)
        qk = qk_ref[0, rows, :].astype(F32)
        q = qk[:, :QK_W] * (GLA_DK ** -0.5)
        k = qk[:, QK_W:]
        q_in = (q * jnp.exp(bcum)).astype(BF)
        k_in = (k * jnp.exp(-bcum)).astype(BF)
        k_st = (k * jnp.exp(blast - bcum)).astype(BF)
        decay = jnp.exp(blast)
        qs = jnp.concatenate(
            [jnp.where(lane_head == hh, q_in, jnp.zeros_like(q_in)) for hh in range(GLA_HEADS)], axis=0)
        a = _dot_nt(qs, k_in)
        tri4 = jnp.concatenate([tri] * GLA_HEADS, axis=0)
        a = jnp.where(tri4, a, 0.0).astype(BF)
        v = v_ref[0, rows, :]
        o_intra = jnp.concatenate(
            [_dot(a[PAIR * hh:PAIR * (hh + 1)], v[:, GLA_DV * hh:GLA_DV * (hh + 1)])
             for hh in range(GLA_HEADS)], axis=1)
        vt = vt_ref[0, :, rows]
        for ci in ((0, 1) if forward else (1, 0)):
            lo_r, hi_r = GLA_CHUNK * ci, GLA_CHUNK * (ci + 1)
            st = s_ref[sidx]
            o_inter = _dot_nt(q_in[lo_r:hi_r], st.astype(BF))
            o_ref[0, pl.ds(start + lo_r, GLA_CHUNK), :] = o_intra[lo_r:hi_r] + o_inter
            k_c = jnp.where(row_chunk == ci, k_st, jnp.zeros_like(k_st))
            ut = _dot(vt, k_c)
            s_ref[sidx] = st * decay[lo_r:lo_r + 1, :] + jnp.where(bd_mask, ut, 0.0)

    def body(p, carry):
        pair_dir(qkf_ref, vf_ref, vtf_ref, laf_ref, of_ref, 0, p, True)
        pair_dir(qkb_ref, vb_ref, vtb_ref, lab_ref, ob_ref, 1, npairs - 1 - p, False)
        return carry

    lax.fori_loop(0, npairs, body, 0)

    @pl.when(n == pl.num_programs(1) - 1)
    def _():
        sfin_ref[0] = s_ref[...]


def _gla_call(qk, v, vt, la, s0, tb):
    b, l, _ = qk.shape
    nb = l // tb
    fwd = lambda c: pl.BlockSpec((1, tb, c), lambda bi, i: (bi, i, 0))
    bwd = lambda c: pl.BlockSpec((1, tb, c), lambda bi, i: (bi, nb - 1 - i, 0))
    st_spec = pl.BlockSpec((1, 2, V_W, QK_W), lambda bi, i: (bi, 0, 0, 0))
    return pl.pallas_call(
        functools.partial(_gla_kernel, tb=tb),
        grid=(b, nb),
        in_specs=[fwd(2 * QK_W), bwd(2 * QK_W), fwd(V_W), bwd(V_W),
                  pl.BlockSpec((1, V_W, tb), lambda bi, i: (bi, 0, i)),
                  pl.BlockSpec((1, V_W, tb), lambda bi, i: (bi, 0, nb - 1 - i)),
                  pl.BlockSpec((1, tb, QK_W), lambda bi, i: (bi, i, 0)),
                  pl.BlockSpec((1, tb, QK_W), lambda bi, i: (bi, nb - 1 - i, 1)),
                  st_spec],
        out_specs=(fwd(V_W), bwd(V_W), st_spec),
        out_shape=(jax.ShapeDtypeStruct((b, l, V_W), F32),
                   jax.ShapeDtypeStruct((b, l, V_W), F32),
                   jax.ShapeDtypeStruct((b, 2, V_W, QK_W), F32)),
        scratch_shapes=[pltpu.VMEM((2, V_W, QK_W), F32)],
        compiler_params=_params(("parallel", "arbitrary")),
        name="gla",
    )(qk, qk, v, v, vt, vt, la, la, s0)


def _attn_kernel(qt_ref, k_ref, ct_ref, wuv_ref, o_ref, m_ref, acc_ref, ot_ref, s_ref, p_ref, *, tk):
    tq = qt_ref.shape[2]
    nchunks = k_ref.shape[1] // tk
    nsub = tk // KEY_SUB
    m_ref[...] = jnp.full(m_ref.shape, NEG_BIG, F32)
    acc_ref[...] = jnp.zeros(acc_ref.shape, F32)

    def score_sub(kstart, hh, i, slot, mpart):
        ks = k_ref[0, pl.ds(kstart + KEY_SUB * i, KEY_SUB), :]
        s = _dot(ks, qt_ref[0, Q_ABS * hh:Q_ABS * (hh + 1), :])
        s_ref[slot, KEY_SUB * i:KEY_SUB * (i + 1), :] = s
        return jnp.maximum(mpart, jnp.max(s.reshape(KEY_SUB // 8, 8, tq), axis=0))

    def score_finish(hh, mpart):
        m_old = m_ref[hh:hh + 1, :]
        m_new = jnp.maximum(m_old, jnp.max(mpart, axis=0, keepdims=True))
        m_ref[hh:hh + 1, :] = m_new
        return m_new, jnp.exp2(m_old - m_new)

    def value_sub(i, slot, pslot, m_new):
        rows = slice(KEY_SUB * i, KEY_SUB * (i + 1))
        p_ref[pslot, rows, :] = jnp.exp2(s_ref[slot, rows, :] - m_new).astype(BF)

    mpart0 = jnp.full((8, tq), NEG_BIG, F32)
    ahead = []
    for hh in range(ATTN_SKEW):
        mp = mpart0
        for i in range(nsub):
            mp = score_sub(0, hh, i, hh % SCORE_SLOTS, mp)
        ahead.extend(score_finish(hh, mp))

    def chunk_body(j, carry):
        pend = [(carry[2 * u], carry[2 * u + 1]) for u in range(ATTN_SKEW)]
        start = pl.multiple_of(j * tk, KEY_SUB)
        start_next = pl.multiple_of(jnp.minimum(j + 1, nchunks - 1) * tk, KEY_SUB)
        for hh in range(MLA_HEADS):
            m_cur, alpha_cur = pend.pop(0)
            nh = (hh + ATTN_SKEW) % MLA_HEADS
            nstart = start if hh + ATTN_SKEW < MLA_HEADS else start_next
            mp = mpart0
            for i in range(nsub):
                value_sub(i, hh % SCORE_SLOTS, hh % 2, m_cur)
                mp = score_sub(nstart, nh, i, (hh + ATTN_SKEW) % SCORE_SLOTS, mp)
            cc = ct_ref[0, :, pl.ds(start, tk)]
            acc_ref[hh] = alpha_cur * acc_ref[hh] + _dot(cc, p_ref[hh % 2])
            pend.append(score_finish(nh, mp))
        return tuple(v for pair in pend for v in pair)

    lax.fori_loop(0, nchunks, chunk_body, tuple(ahead))
    for hh in range(MLA_HEADS):
        a = acc_ref[hh]
        o_lat = (a[:MLA_KV_LORA] / a[MLA_KV_LORA:MLA_KV_LORA + 1]).astype(BF)
        ot_ref[MLA_DV * hh:MLA_DV * (hh + 1), :] = _dot(wuv_ref[hh], o_lat)
    o_ref[0] = ot_ref[...].T.astype(BF)


def _attn_call(qt, kabs, ct, wuv, tq, tk):
    b, _, l = qt.shape
    lk = kabs.shape[1]
    ct1 = jnp.concatenate([ct, jnp.ones((b, ONES_ROWS, lk), BF)], axis=1)
    vrows = MLA_KV_LORA + ONES_ROWS
    return pl.pallas_call(
        functools.partial(_attn_kernel, tk=tk),
        grid=(b, l // tq),
        in_specs=[pl.BlockSpec((1, MLA_HEADS * Q_ABS, tq), lambda bi, i: (bi, 0, i)),
                  pl.BlockSpec((1, lk, Q_ABS), lambda bi, i: (bi, 0, 0)),
                  pl.BlockSpec((1, vrows, lk), lambda bi, i: (bi, 0, 0)),
                  _const_spec(wuv.shape)],
        out_specs=pl.BlockSpec((1, tq, MLA_HEADS * MLA_DV), lambda bi, i: (bi, i, 0)),
        out_shape=jax.ShapeDtypeStruct((b, l, MLA_HEADS * MLA_DV), BF),
        scratch_shapes=[pltpu.VMEM((MLA_HEADS, tq), F32),
                        pltpu.VMEM((MLA_HEADS, vrows, tq), F32),
                        pltpu.VMEM((MLA_HEADS * MLA_DV, tq), F32),
                        pltpu.VMEM((SCORE_SLOTS, tk, tq), F32),
                        pltpu.VMEM((2, tk, tq), BF)],
        compiler_params=_params(("parallel", "parallel")),
        name="attn",
    )(qt, kabs, ct1, wuv)


def _shift_rows(t, prev_row, next_row):
    n = t.shape[0]
    row = lax.broadcasted_iota(jnp.int32, t.shape, 0)
    down = jnp.where(row == 0, prev_row, pltpu.roll(t, 1, 0))
    up = jnp.where(row == n - 1, next_row, pltpu.roll(t, n - 1, 0))
    return down, up


def _merge_kernel(x_ref, mod_ref, g1_ref, of_ref, ob_ref, oat_ref, u_ref, up_ref, un_ref,
                  wrsg_ref, glag_ref, wbra_ref, wbrb_ref, wbrc_ref, scw_ref, wo_ref, xo_ref):
    i = pl.program_id(1)
    x = x_ref[0]
    m = mod_ref[0]
    d = x.shape[1]
    hb = (_rms(x, g1_ref[...]) * (1.0 + m[1:2]) + m[0:1]).astype(BF)
    rsg = _dot(hb, wrsg_ref[...])
    r_a = rsg[:, :V_W]
    sb = rsg[:, V_W:V_W + SC_W]
    gates = jax.nn.sigmoid(rsg[:, V_W + SC_W:])

    o = of_ref[0] + ob_ref[0]
    gg = glag_ref[...]
    on = jnp.concatenate(
        [_rms(o[:, GLA_DV * hh:GLA_DV * (hh + 1)], gg[:, GLA_DV * hh:GLA_DV * (hh + 1)])
         for hh in range(GLA_HEADS)], axis=1)
    y_a = _dot((on * _silu(r_a)).astype(BF), wbra_ref[...])
    y_b = _dot(oat_ref[0], wbrb_ref[...])

    u = u_ref[0]
    prev_row = jnp.where(i > 0, up_ref[0, HALO - 1:HALO, :], 0.0)
    next_row = jnp.where(i < pl.num_programs(1) - 1, un_ref[0, 0:1, :], 0.0)
    u_dn, u_up = _shift_rows(u, prev_row, next_row)
    w = scw_ref[...]
    conv = u_dn * w[0:1] + u * w[1:2] + u_up * w[2:3]
    y_c = _dot((sb * conv).astype(BF), wbrc_ref[...])

    mix = gates[:, :d] * y_a + gates[:, d:2 * d] * y_b + gates[:, 2 * d:] * y_c
    xo_ref[0] = x + m[2:3] * _dot(mix.astype(BF), wo_ref[...])


def _halo_specs(tm, l, c):
    nb8 = tm // HALO
    last8 = l // HALO - 1
    prev = pl.BlockSpec((1, HALO, c), lambda bi, i: (bi, jnp.maximum(i * nb8 - 1, 0), 0))
    nxt = pl.BlockSpec((1, HALO, c), lambda bi, i: (bi, jnp.minimum((i + 1) * nb8, last8), 0))
    return prev, nxt


def _merge_call(x, mod, o_f, o_b, o_att, u, lw, tm):
    b, l, d = x.shape
    tok = lambda c: pl.BlockSpec((1, tm, c), lambda bi, i: (bi, i, 0))
    up_spec, un_spec = _halo_specs(tm, l, SC_W)
    weights = [lw["wrsg"], lw["glag"], lw["wbra"], lw["wbrb"], lw["wbrc"], lw["scw"], lw["wo"]]
    return pl.pallas_call(
        _merge_kernel,
        grid=(b, l // tm),
        in_specs=[tok(d), pl.BlockSpec((1, 6, d), lambda bi, i: (bi, 0, 0)), _const_spec(lw["g1"].shape),
                  tok(V_W), tok(V_W), tok(MLA_HEADS * MLA_DV), tok(SC_W), up_spec, un_spec]
                 + [_const_spec(w.shape) for w in weights],
        out_specs=tok(d),
        out_shape=jax.ShapeDtypeStruct((b, l, d), F32),
        compiler_params=_params(("parallel", "parallel")),
        name="merge",
    )(x, mod, lw["g1"], o_f, o_b, o_att, u, u, u, *weights)


def _ffn_kernel(x_ref, xp_ref, xn_ref, mod_ref, g2_ref, wg_ref, wu_ref, cw_ref, wd_ref, gf_ref,
                xo_ref, *, fc, final):
    i = pl.program_id(1)
    x = x_ref[0]
    m = mod_ref[0]
    tm = x.shape[0]
    nf = wg_ref.shape[1]

    def hmod(t):
        return _rms(t, g2_ref[...]) * (1.0 + m[4:5]) + m[3:4]

    h = hmod(x)
    hb = h.astype(BF)
    h_ext = jnp.concatenate([hmod(xp_ref[0]), h, hmod(xn_ref[0])], axis=0).astype(BF)
    has_prev = i > 0
    has_next = i < pl.num_programs(1) - 1
    row = lax.broadcasted_iota(jnp.int32, (tm, fc), 0)

    acc = jnp.zeros((tm, x.shape[1]), F32)
    for ci in range(nf // fc):
        cols = slice(ci * fc, (ci + 1) * fc)
        g_ext = _dot(h_ext, wg_ref[:, cols])
        g_mid = g_ext[HALO:HALO + tm]
        g_dn = g_ext[HALO - 1:HALO - 1 + tm]
        g_up = g_ext[HALO + 1:HALO + 1 + tm]
        g_dn = jnp.where(jnp.logical_and(row == 0, jnp.logical_not(has_prev)), 0.0, g_dn)
        g_up = jnp.where(jnp.logical_and(row == tm - 1, jnp.logical_not(has_next)), 0.0, g_up)
        cw = cw_ref[:, cols]
        g = g_dn * cw[0:1] + g_mid * cw[1:2] + g_up * cw[2:3] + cw[3:4]
        a = (_silu(g) * _dot(hb, wu_ref[:, cols])).astype(BF)
        acc = acc + _dot(a, wd_ref[cols, :])
    y = x + m[5:6] * acc
    if final:
        y = _rms(y, gf_ref[...])
    xo_ref[0] = y


def _ffn_call(x, mod, lw, gfinal, tm, final):
    b, l, d = x.shape
    nf = lw["wg"].shape[1]
    fc = 256
    tok = pl.BlockSpec((1, tm, d), lambda bi, i: (bi, i, 0))
    xp_spec, xn_spec = _halo_specs(tm, l, d)
    weights = [lw["g2"], lw["wg"], lw["wu"], lw["cw"], lw["wd"], gfinal]
    return pl.pallas_call(
        functools.partial(_ffn_kernel, fc=fc, final=final),
        grid=(b, l // tm),
        in_specs=[tok, xp_spec, xn_spec, pl.BlockSpec((1, 6, d), lambda bi, i: (bi, 0, 0))]
                 + [_const_spec(w.shape) for w in weights],
        out_specs=tok,
        out_shape=jax.ShapeDtypeStruct((b, l, d), F32),
        compiler_params=_params(("parallel", "parallel")),
        name="ffn",
    )(x, x, x, mod, *weights)


def _rope_perm():
    idx = np.zeros((MLA_ROPE,), np.int32)
    sgn = np.zeros((MLA_ROPE,), np.float32)
    for ax in range(2):
        for f in range(ROPE_FREQS):
            lo = ax * 2 * ROPE_FREQS + f
            hi = lo + ROPE_FREQS
            idx[lo], sgn[lo] = hi, -1.0
            idx[hi], sgn[hi] = lo, 1.0
    return idx, sgn


def _rope_tables(l, rotate):
    if rotate:
        rows = l // GRID_W
        row = jnp.repeat(jnp.arange(rows, dtype=F32), GRID_W)
        col = jnp.tile(jnp.arange(GRID_W, dtype=F32), rows)
        inv = ROPE_THETA ** (-jnp.arange(ROPE_FREQS, dtype=F32) / ROPE_FREQS)
        ang = jnp.stack([row[:, None] * inv, col[:, None] * inv], axis=1)
        cos, sin = jnp.cos(ang), jnp.sin(ang)
        expand = lambda t: jnp.broadcast_to(t[:, :, None, :], (l, 2, 2, ROPE_FREQS)).reshape(l, MLA_ROPE)
        cos32, sin32 = expand(cos), expand(sin)
    else:
        cos32, sin32 = jnp.ones((l, MLA_ROPE), F32), jnp.zeros((l, MLA_ROPE), F32)
    cosq = jnp.tile(cos32, (1, MLA_HEADS)).T
    sinq = jnp.tile(sin32, (1, MLA_HEADS)).T
    pad = lambda t: jnp.pad(t, ((0, 0), (0, 128 - MLA_ROPE)))
    return cosq, sinq, pad(cos32), pad(sin32)


def _layer_weights(l, p):
    d = p["w_in"].shape[1]
    w_in = p["w_in"][l]
    sizes = (QK_W, QK_W, V_W, V_W, 2 * GLA_RANK, MLA_Q_LORA, MLA_KV_LORA, MLA_ROPE, SC_W, SC_W, SC_W,
             N_BRANCH * d)
    offs = np.concatenate([[0], np.cumsum(sizes)])
    col = lambda a, b: w_in[:, int(offs[a]):int(offs[b])]
    idx, sgn = _rope_perm()
    w_kr = col(7, 8)
    w_kr_rot = w_kr[:, idx] * sgn
    zeros = lambda n: jnp.zeros((d, n), F32)
    wgk = jnp.zeros((128, 2 * QK_W), F32)
    wgk = wgk.at[MLA_ROPE:MLA_ROPE + GLA_RANK, :QK_W].set(p["w_gk2"][l, 0])
    wgk = wgk.at[MLA_ROPE + GLA_RANK:MLA_ROPE + 2 * GLA_RANK, QK_W:].set(p["w_gk2"][l, 1])
    w_uq = p["w_uq"][l].reshape(MLA_Q_LORA, MLA_HEADS, MLA_NOPE + MLA_ROPE)
    w_uq_rope = w_uq[:, :, MLA_NOPE:]
    w_ukv = p["w_ukv"][l].reshape(MLA_KV_LORA, MLA_HEADS, MLA_NOPE + MLA_DV)
    nf = p["w_ffn_gate"].shape[2]
    cw = jnp.concatenate([p["ffn_conv_w"][l], p["ffn_conv_b"][l][None], jnp.zeros((4, nf), F32)], axis=0)
    scw = jnp.concatenate([p["sc_w"][l], jnp.zeros((5, SC_W), F32)], axis=0)
    bf = lambda t: t.astype(BF)
    return {
        "g1": p["norm1_g"][l][None],
        "wqkv": bf(col(0, 3)),
        "wsm1": bf(jnp.concatenate([w_kr, col(4, 5), zeros(128 - MLA_ROPE - 2 * GLA_RANK)], axis=1)),
        "wsm2": bf(jnp.concatenate([w_kr_rot, zeros(128 - MLA_ROPE)], axis=1)),
        "wgk": bf(wgk),
        "bgk": p["b_gk2"][l].reshape(1, 2 * QK_W),
        "wcq": bf(col(5, 6)),
        "gq": p["mla_q_norm_g"][l][None],
        "wuqn": bf(w_uq[:, :, :MLA_NOPE].reshape(MLA_Q_LORA, -1).T),
        "wuk": bf(jnp.transpose(w_ukv[:, :, :MLA_NOPE], (1, 0, 2))),
        "wuqr": bf(w_uq_rope.reshape(MLA_Q_LORA, -1).T),
        "wuqrr": bf((w_uq_rope[:, :, idx] * sgn).reshape(MLA_Q_LORA, -1).T),
        "wckv": bf(col(6, 7)),
        "gkv": p["mla_kv_norm_g"][l][None],
        "ws": bf(col(9, 11)),
        "wuv": bf(jnp.transpose(w_ukv[:, :, MLA_NOPE:], (1, 2, 0))),
        "wrsg": bf(jnp.concatenate([col(3, 4), col(8, 9), col(11, 12)], axis=1)),
        "glag": p["gla_norm_g"][l][None],
        "wbra": bf(p["w_br_a"][l]),
        "wbrb": bf(p["w_br_b"][l]),
        "wbrc": bf(p["w_br_c"][l]),
        "scw": scw,
        "wo": bf(p["w_o"][l]),
        "g2": p["norm2_g"][l][None],
        "wg": bf(p["w_ffn_gate"][l]),
        "wu": bf(p["w_ffn_up"][l]),
        "cw": cw,
        "wd": bf(p["w_ffn_down"][l]),
    }


def _pick(n, options):
    for o in options:
        if n % o == 0:
            return o
    raise ValueError(f"no tile in {options} divides {n}")


def kernel(x, c, ctx, c_ctx, w_ada, b_ada, norm1_g, w_in, w_gk2, b_gk2, gla_norm_g, mla_q_norm_g, w_uq,
           mla_kv_norm_g, w_ukv, sc_w, w_br_a, w_br_b, w_br_c, w_o, norm2_g, w_ffn_gate, w_ffn_up,
           ffn_conv_w, ffn_conv_b, w_ffn_down, final_norm_g):
    p = dict(w_in=w_in, w_gk2=w_gk2, b_gk2=b_gk2, gla_norm_g=gla_norm_g, mla_q_norm_g=mla_q_norm_g,
             w_uq=w_uq, mla_kv_norm_g=mla_kv_norm_g, w_ukv=w_ukv, sc_w=sc_w, w_br_a=w_br_a,
             w_br_b=w_br_b, w_br_c=w_br_c, w_o=w_o, norm1_g=norm1_g, norm2_g=norm2_g,
             w_ffn_gate=w_ffn_gate, w_ffn_up=w_ffn_up, ffn_conv_w=ffn_conv_w, ffn_conv_b=ffn_conv_b,
             w_ffn_down=w_ffn_down)
    b, l, d = x.shape
    lc = ctx.shape[1]
    depth = w_in.shape[0]
    assert l % PAIR == 0 and lc % PAIR == 0 and l % GRID_W == 0

    tiles = (512, 256, 128)
    tm, tm_c = _pick(l, tiles), _pick(lc, tiles)
    tb, tb_c = _pick(l, tiles), _pick(lc, tiles)
    tq, tq_c = _pick(l, tiles), _pick(lc, tiles)
    key_tiles = (3 * KEY_SUB, 2 * KEY_SUB, KEY_SUB)
    tk, tk_c = _pick(l + lc, key_tiles), _pick(lc, key_tiles)

    rows = -(-(b + 1) // 8) * 8
    cc = jnp.concatenate([c, c_ctx[None], jnp.zeros((rows - b - 1, d), F32)], axis=0)
    mod_all = _mod_call(cc, w_ada, b_ada)
    tabs = _rope_tables(l, True)
    tabs_c = _rope_tables(lc, False)
    gfinal = final_norm_g[None]
    zero_state = jnp.zeros((b, 2, V_W, QK_W), F32)

    xc = ctx
    for li in range(depth):
        last = li == depth - 1
        lw = _layer_weights(li, p)
        mod = mod_all[li, :b].reshape(b, 6, d)
        mod_c = jnp.broadcast_to(mod_all[li, b].reshape(1, 6, d), (b, 6, d))

        qk_c, v_c, vt_c, la_c, qt_c, kabs_c, ct_c, u_c = _proj_call(xc, mod_c, lw, tabs_c, tm_c)
        of_c, ob_c, s_ctx = _gla_call(qk_c, v_c, vt_c, la_c, zero_state, tb_c)

        qk, v, vt, la, qt, kabs, ct, u = _proj_call(x, mod, lw, tabs, tm)
        o_f, o_b, _ = _gla_call(qk, v, vt, la, s_ctx, tb)
        k_all = jnp.concatenate([kabs, kabs_c], axis=1)
        ct_all = jnp.concatenate([ct, ct_c], axis=2)
        o_att = _attn_call(qt, k_all, ct_all, lw["wuv"], tq, tk)
        x = _merge_call(x, mod, o_f, o_b, o_att, u, lw, tm)
        if not last:
            o_att_c = _attn_call(qt_c, kabs_c, ct_c, lw["wuv"], tq_c, tk_c)
            xc = _merge_call(xc, mod_c, of_c, ob_c, o_att_c, u_c, lw, tm_c)
        x = _ffn_call(x, mod, lw, gfinal, tm, last)
        if not last:
            xc = _ffn_call(xc, mod_c, lw, gfinal, tm_c, False)
    return x
```

```python
import functools

import numpy as np
import jax
import jax.numpy as jnp
from jax import lax
from jax.experimental import pallas as pl
from jax.experimental.pallas import tpu as pltpu

F32 = jnp.float32
BF = jnp.bfloat16

GRID_W = 64
GLA_HEADS = 4
GLA_DK = 64
GLA_DV = 128
GLA_RANK = 16
GLA_GATE_NORM = 16.0
GLA_CHUNK = 64
MLA_HEADS = 8
MLA_Q_LORA = 256
MLA_KV_LORA = 128
MLA_NOPE = 64
MLA_ROPE = 32
MLA_DV = 64
ROPE_THETA = 10000.0
ROPE_FREQS = MLA_ROPE // 4
SC_W = 512
NORM_EPS = 1e-6
N_BRANCH = 3

QK_W = GLA_HEADS * GLA_DK
V_W = GLA_HEADS * GLA_DV
Q_ABS = MLA_KV_LORA + MLA_ROPE
PAIR = 2 * GLA_CHUNK
HALO = 8
NEG_BIG = -1e30
ONES_ROWS = 16
LOG2E = 1.4426950408889634
KEY_SUB = 256
ATTN_SKEW = 2
SCORE_SLOTS = 4
assert MLA_HEADS % SCORE_SLOTS == 0 and SCORE_SLOTS > ATTN_SKEW

VMEM_LIMIT = 56 * 1024 * 1024


def _dot(a, b):
    return jnp.dot(a, b, preferred_element_type=F32)


def _dot_nt(a, b):
    return lax.dot_general(a, b, (((1,), (1,)), ((), ())), preferred_element_type=F32)


def _rms(x, g):
    return x * lax.rsqrt(jnp.mean(x * x, axis=-1, keepdims=True) + NORM_EPS) * g


def _silu(x):
    return x * jax.nn.sigmoid(x)


def _const_spec(shape):
    nd = len(shape)
    return pl.BlockSpec(shape, lambda *_: (0,) * nd, pipeline_mode=pl.Buffered(1))


def _params(sem):
    return pltpu.CompilerParams(dimension_semantics=sem, vmem_limit_bytes=VMEM_LIMIT)


def _mod_kernel(c_ref, w_ref, b_ref, o_ref):
    a = _silu(c_ref[...]).astype(BF)
    o_ref[0] = _dot(a, w_ref[0].astype(BF)) + b_ref[0]


def _mod_call(cc, w_ada, b_ada):
    nl, d, n = w_ada.shape
    rows = cc.shape[0]
    tn = 1024
    return pl.pallas_call(
        _mod_kernel,
        grid=(nl, n // tn),
        in_specs=[
            pl.BlockSpec((rows, d), lambda l, j: (0, 0)),
            pl.BlockSpec((1, d, tn), lambda l, j: (l, 0, j)),
            pl.BlockSpec((1, 1, tn), lambda l, j: (l, 0, j)),
        ],
        out_specs=pl.BlockSpec((1, rows, tn), lambda l, j: (l, 0, j)),
        out_shape=jax.ShapeDtypeStruct((nl, rows, n), F32),
        compiler_params=_params(("parallel", "parallel")),
        name="adaln_mod",
    )(cc, w_ada, b_ada.reshape(nl, 1, n))


def _proj_kernel(x_ref, mod_ref, g1_ref, wqkv_ref, wsm1_ref, wsm2_ref, wgk_ref, bgk_ref,
                 wcq_ref, gq_ref, wuqn_ref, wuk_ref, wuqr_ref, wuqrr_ref, wckv_ref, gkv_ref,
                 ws_ref, cosq_ref, sinq_ref, cosk_ref, sink_ref,
                 qk_ref, v_ref, vt_ref, la_ref, qt_ref, kabs_ref, ct_ref, u_ref):
    x = x_ref[0]
    m = mod_ref[0]
    h = _rms(x, g1_ref[...]) * (1.0 + m[1:2]) + m[0:1]
    hb = h.astype(BF)

    sm1 = _dot(hb, wsm1_ref[...])
    sm2 = _dot(hb, wsm2_ref[...])
    cq = _dot(hb, wcq_ref[...])
    ckv = _dot(hb, wckv_ref[...])

    qkv = _dot(hb, wqkv_ref[...])

    z = _dot(sm1.astype(BF), wgk_ref[...]) + bgk_ref[...]
    la_ref[0] = jax.nn.log_sigmoid(z) * (1.0 / GLA_GATE_NORM)

    scale = (MLA_NOPE + MLA_ROPE) ** -0.5 * LOG2E
    cqn = _rms(cq, gq_ref[...]).astype(BF)
    qnt = _dot_nt(wuqn_ref[...], cqn).astype(BF)
    qr_t = _dot_nt(wuqr_ref[...], cqn)
    qrr_t = _dot_nt(wuqrr_ref[...], cqn)

    s = _dot(hb, ws_ref[...])

    qk_ref[0] = qkv[:, :2 * QK_W].astype(BF)
    v = qkv[:, 2 * QK_W:]
    v_ref[0] = v.astype(BF)
    vt_ref[0] = v.T.astype(BF)

    qrope_t = (qr_t * cosq_ref[...] + qrr_t * sinq_ref[...]) * scale
    for hh in range(MLA_HEADS):
        qa = _dot(wuk_ref[hh], qnt[MLA_NOPE * hh:MLA_NOPE * (hh + 1), :]) * scale
        qt_ref[0, Q_ABS * hh:Q_ABS * hh + MLA_KV_LORA, :] = qa.astype(BF)
        qt_ref[0, Q_ABS * hh + MLA_KV_LORA:Q_ABS * (hh + 1), :] = (
            qrope_t[MLA_ROPE * hh:MLA_ROPE * (hh + 1), :].astype(BF))

    cn = _rms(ckv, gkv_ref[...])
    kabs_ref[0, :, 0:MLA_KV_LORA] = cn.astype(BF)
    kr = sm1 * cosk_ref[...] + sm2 * sink_ref[...]
    kabs_ref[0, :, MLA_KV_LORA:Q_ABS] = kr[:, 0:MLA_ROPE].astype(BF)
    ct_ref[0] = cn.T.astype(BF)

    u_ref[0] = s[:, :SC_W] * s[:, SC_W:]


def _proj_call(x, mod, lw, tabs, tm):
    b, l, d = x.shape
    cosq, sinq, cosk, sink = tabs
    tok = lambda c: pl.BlockSpec((1, tm, c), lambda bi, i: (bi, i, 0))
    tr = lambda r: pl.BlockSpec((1, r, tm), lambda bi, i: (bi, 0, i))
    weights = [lw["g1"], lw["wqkv"], lw["wsm1"], lw["wsm2"], lw["wgk"], lw["bgk"], lw["wcq"],
               lw["gq"], lw["wuqn"], lw["wuk"], lw["wuqr"], lw["wuqrr"], lw["wckv"], lw["gkv"],
               lw["ws"]]
    in_specs = ([tok(d), pl.BlockSpec((1, 6, d), lambda bi, i: (bi, 0, 0))]
                + [_const_spec(w.shape) for w in weights]
                + [pl.BlockSpec((MLA_HEADS * MLA_ROPE, tm), lambda bi, i: (0, i)),
                   pl.BlockSpec((MLA_HEADS * MLA_ROPE, tm), lambda bi, i: (0, i)),
                   pl.BlockSpec((tm, 128), lambda bi, i: (i, 0)),
                   pl.BlockSpec((tm, 128), lambda bi, i: (i, 0))])
    out_shape = (
        jax.ShapeDtypeStruct((b, l, 2 * QK_W), BF),
        jax.ShapeDtypeStruct((b, l, V_W), BF),
        jax.ShapeDtypeStruct((b, V_W, l), BF),
        jax.ShapeDtypeStruct((b, l, 2 * QK_W), F32),
        jax.ShapeDtypeStruct((b, MLA_HEADS * Q_ABS, l), BF),
        jax.ShapeDtypeStruct((b, l, Q_ABS), BF),
        jax.ShapeDtypeStruct((b, MLA_KV_LORA, l), BF),
        jax.ShapeDtypeStruct((b, l, SC_W), F32),
    )
    out_specs = (tok(2 * QK_W), tok(V_W), tr(V_W), tok(2 * QK_W), tr(MLA_HEADS * Q_ABS),
                 tok(Q_ABS), tr(MLA_KV_LORA), tok(SC_W))
    return pl.pallas_call(
        _proj_kernel,
        grid=(b, l // tm),
        in_specs=in_specs,
        out_specs=out_specs,
        out_shape=out_shape,
        compiler_params=_params(("parallel", "parallel")),
        name="proj",
    )(x, mod, *weights, cosq, sinq, cosk, sink)


def _gla_kernel(qkf_ref, qkb_ref, vf_ref, vb_ref, vtf_ref, vtb_ref, laf_ref, lab_ref, s0_ref,
                of_ref, ob_ref, sfin_ref, s_ref, *, tb):
    n = pl.program_id(1)
    npairs = tb // PAIR

    @pl.when(n == 0)
    def _():
        s_ref[...] = s0_ref[0]

    r = lax.broadcasted_iota(jnp.int32, (PAIR, PAIR), 0)
    c = lax.broadcasted_iota(jnp.int32, (PAIR, PAIR), 1)
    same = (r // GLA_CHUNK) == (c // GLA_CHUNK)
    ones_c = jnp.where(same, 1.0, 0.0).astype(BF)
    lane_head = lax.broadcasted_iota(jnp.int32, (PAIR, QK_W), 1) // GLA_DK
    row_chunk = lax.broadcasted_iota(jnp.int32, (PAIR, QK_W), 0) // GLA_CHUNK
    bd_mask = (lax.broadcasted_iota(jnp.int32, (V_W, QK_W), 0) // GLA_DV
               == lax.broadcasted_iota(jnp.int32, (V_W, QK_W), 1) // GLA_DK)

    def stage_decay(d):
        tri_b = jnp.where(d["tri"], 1.0, 0.0).astype(BF)
        la = d["la_ref"][0, d["rows"], :]
        hi = la.astype(BF)
        lo = (la - hi.astype(F32)).astype(BF)
        d["bcum"] = _dot(tri_b, hi) + _dot(tri_b, lo)
        d["blast"] = _dot(ones_c, hi) + _dot(ones_c, lo)

    def stage_scale(d):
        qk = d["qk_ref"][0, d["rows"], :].astype(F32)
        q = qk[:, :QK_W] * (GLA_DK ** -0.5)
        k = qk[:, QK_W:]
        late = row_chunk == (1 if d["forward"] else 0)
        blast = d["blast"]
        other = jnp.concatenate([blast[GLA_CHUNK:], blast[:GLA_CHUNK]], axis=0)
        d_other = jnp.exp(other)
        q_in = q * jnp.exp(d["bcum"])
        k_st = k * jnp.exp(blast - d["bcum"])
        d["q_in"] = q_in.astype(BF)
        d["k_in"] = (k * jnp.exp(-d["bcum"])).astype(BF)
        d["k_st"] = k_st.astype(BF)
        d["q_state"] = jnp.where(late, q_in * d_other, q_in).astype(BF)
        d["k_state"] = jnp.where(late, k_st, k_st * d_other).astype(BF)
        d["decay"] = jnp.exp(blast[0:1] + other[0:1])

    def stage_state_inc(d):
        vt = d["vt_ref"][0, :, d["rows"]]
        d["ut"] = jnp.where(bd_mask, _dot(vt, d["k_state"]), 0.0)

    def stage_scores(d):
        q_in = d["q_in"]
        qs = jnp.concatenate(
            [jnp.where(lane_head == hh, q_in, jnp.zeros_like(q_in)) for hh in range(GLA_HEADS)], axis=0)
        a2 = _dot_nt(qs, jnp.concatenate([d["k_in"], d["k_st"]], axis=0))
        cross = ((r >= GLA_CHUNK) & (c < GLA_CHUNK)) if d["forward"] else ((r < GLA_CHUNK) & (c >= GLA_CHUNK))
        tri4 = jnp.concatenate([d["tri"]] * GLA_HEADS, axis=0)
        cross4 = jnp.concatenate([cross] * GLA_HEADS, axis=0)
        d["a"] = jnp.where(tri4, a2[:, :PAIR], jnp.where(cross4, a2[:, PAIR:], 0.0)).astype(BF)

    def stage_intra(d):
        v = d["v_ref"][0, d["rows"], :]
        a = d["a"]
        d["o_intra"] = jnp.concatenate(
            [_dot(a[PAIR * hh:PAIR * (hh + 1)], v[:, GLA_DV * hh:GLA_DV * (hh + 1)])
             for hh in range(GLA_HEADS)], axis=1)

    def stage_scan(d):
        st = s_ref[d["sidx"]]
        d["o_ref"][0, d["rows"], :] = d["o_intra"] + _dot_nt(d["q_state"], st.astype(BF))
        s_ref[d["sidx"]] = st * d["decay"] + d["ut"]

    def body(p, carry):
        dirs = [dict(qk_ref=qkf_ref, v_ref=vf_ref, vt_ref=vtf_ref, la_ref=laf_ref, o_ref=of_ref, sidx=0,
                     forward=True, start=pl.multiple_of(p * PAIR, PAIR)),
                dict(qk_ref=qkb_ref, v_ref=vb_ref, vt_ref=vtb_ref, la_ref=lab_ref, o_ref=ob_ref, sidx=1,
                     forward=False, start=pl.multiple_of((npairs - 1 - p) * PAIR, PAIR))]
        for d in dirs:
            d["rows"] = pl.ds(d["start"], PAIR)
            d["tri"] = jnp.logical_and(same, (c <= r) if d["forward"] else (c >= r))
        for stage in (stage_decay, stage_scale, stage_state_inc, stage_scores, stage_intra, stage_scan):
            for d in dirs:
                stage(d)
        return carry

    lax.fori_loop(0, npairs, body, 0)

    @pl.when(n == pl.num_programs(1) - 1)
    def _():
        sfin_ref[0] = s_ref[...]


def _gla_call(qk, v, vt, la, s0, tb):
    b, l, _ = qk.shape
    nb = l // tb
    fwd = lambda c: pl.BlockSpec((1, tb, c), lambda bi, i: (bi, i, 0))
    bwd = lambda c: pl.BlockSpec((1, tb, c), lambda bi, i: (bi, nb - 1 - i, 0))
    st_spec = pl.BlockSpec((1, 2, V_W, QK_W), lambda bi, i: (bi, 0, 0, 0))
    return pl.pallas_call(
        functools.partial(_gla_kernel, tb=tb),
        grid=(b, nb),
        in_specs=[fwd(2 * QK_W), bwd(2 * QK_W), fwd(V_W), bwd(V_W),
                  pl.BlockSpec((1, V_W, tb), lambda bi, i: (bi, 0, i)),
                  pl.BlockSpec((1, V_W, tb), lambda bi, i: (bi, 0, nb - 1 - i)),
                  pl.BlockSpec((1, tb, QK_W), lambda bi, i: (bi, i, 0)),
                  pl.BlockSpec((1, tb, QK_W), lambda bi, i: (bi, nb - 1 - i, 1)),
                  st_spec],
        out_specs=(fwd(V_W), bwd(V_W), st_spec),
        out_shape=(jax.ShapeDtypeStruct((b, l, V_W), F32),
                   jax.ShapeDtypeStruct((b, l, V_W), F32),
                   jax.ShapeDtypeStruct((b, 2, V_W, QK_W), F32)),
        scratch_shapes=[pltpu.VMEM((2, V_W, QK_W), F32)],
        compiler_params=_params(("parallel", "arbitrary")),
        name="gla",
    )(qk, qk, v, v, vt, vt, la, la, s0)


def _attn_kernel(qt_ref, k_ref, ct_ref, wuv_ref, o_ref, m_ref, acc_ref, ot_ref, s_ref, p_ref, *, tk):
    tq = qt_ref.shape[2]
    nchunks = k_ref.shape[1] // tk
    nsub = tk // KEY_SUB
    m_ref[...] = jnp.full(m_ref.shape, NEG_BIG, F32)
    acc_ref[...] = jnp.zeros(acc_ref.shape, F32)

    def score_sub(kstart, hh, i, slot, mpart):
        ks = k_ref[0, pl.ds(kstart + KEY_SUB * i, KEY_SUB), :]
        s = _dot(ks, qt_ref[0, Q_ABS * hh:Q_ABS * (hh + 1), :])
        s_ref[slot, KEY_SUB * i:KEY_SUB * (i + 1), :] = s
        return jnp.maximum(mpart, jnp.max(s.reshape(KEY_SUB // 8, 8, tq), axis=0))

    def score_finish(hh, mpart):
        m_old = m_ref[hh:hh + 1, :]
        m_new = jnp.maximum(m_old, jnp.max(mpart, axis=0, keepdims=True))
        m_ref[hh:hh + 1, :] = m_new
        return m_new, jnp.exp2(m_old - m_new)

    def value_sub(i, slot, pslot, m_new):
        rows = slice(KEY_SUB * i, KEY_SUB * (i + 1))
        p_ref[pslot, rows, :] = jnp.exp2(s_ref[slot, rows, :] - m_new).astype(BF)

    mpart0 = jnp.full((8, tq), NEG_BIG, F32)
    ahead = []
    for hh in range(ATTN_SKEW):
        mp = mpart0
        for i in range(nsub):
            mp = score_sub(0, hh, i, hh % SCORE_SLOTS, mp)
        ahead.extend(score_finish(hh, mp))

    def chunk_body(j, carry):
        pend = [(carry[2 * u], carry[2 * u + 1]) for u in range(ATTN_SKEW)]
        start = pl.multiple_of(j * tk, KEY_SUB)
        start_next = pl.multiple_of(jnp.minimum(j + 1, nchunks - 1) * tk, KEY_SUB)
        for hh in range(MLA_HEADS):
            m_cur, alpha_cur = pend.pop(0)
            nh = (hh + ATTN_SKEW) % MLA_HEADS
            nstart = start if hh + ATTN_SKEW < MLA_HEADS else start_next
            mp = mpart0
            for i in range(nsub):
                value_sub(i, hh % SCORE_SLOTS, hh % 2, m_cur)
                mp = score_sub(nstart, nh, i, (hh + ATTN_SKEW) % SCORE_SLOTS, mp)
            cc = ct_ref[0, :, pl.ds(start, tk)]
            acc_ref[hh] = alpha_cur * acc_ref[hh] + _dot(cc, p_ref[hh % 2])
            pend.append(score_finish(nh, mp))
        return tuple(v for pair in pend for v in pair)

    lax.fori_loop(0, nchunks, chunk_body, tuple(ahead))
    for hh in range(MLA_HEADS):
        a = acc_ref[hh]
        o_lat = (a[:MLA_KV_LORA] / a[MLA_KV_LORA:MLA_KV_LORA + 1]).astype(BF)
        ot_ref[MLA_DV * hh:MLA_DV * (hh + 1), :] = _dot(wuv_ref[hh], o_lat)
    o_ref[0] = ot_ref[...].T.astype(BF)


def _attn_call(qt, kabs, ct, wuv, tq, tk):
    b, _, l = qt.shape
    lk = kabs.shape[1]
    ct1 = jnp.concatenate([ct, jnp.ones((b, ONES_ROWS, lk), BF)], axis=1)
    vrows = MLA_KV_LORA + ONES_ROWS
    return pl.pallas_call(
        functools.partial(_attn_kernel, tk=tk),
        grid=(b, l // tq),
        in_specs=[pl.BlockSpec((1, MLA_HEADS * Q_ABS, tq), lambda bi, i: (bi, 0, i)),
                  pl.BlockSpec((1, lk, Q_ABS), lambda bi, i: (bi, 0, 0)),
                  pl.BlockSpec((1, vrows, lk), lambda bi, i: (bi, 0, 0)),
                  _const_spec(wuv.shape)],
        out_specs=pl.BlockSpec((1, tq, MLA_HEADS * MLA_DV), lambda bi, i: (bi, i, 0)),
        out_shape=jax.ShapeDtypeStruct((b, l, MLA_HEADS * MLA_DV), BF),
        scratch_shapes=[pltpu.VMEM((MLA_HEADS, tq), F32),
                        pltpu.VMEM((MLA_HEADS, vrows, tq), F32),
                        pltpu.VMEM((MLA_HEADS * MLA_DV, tq), F32),
                        pltpu.VMEM((SCORE_SLOTS, tk, tq), F32),
                        pltpu.VMEM((2, tk, tq), BF)],
        compiler_params=_params(("parallel", "parallel")),
        name="attn",
    )(qt, kabs, ct1, wuv)


def _shift_rows(t, prev_row, next_row):
    n = t.shape[0]
    row = lax.broadcasted_iota(jnp.int32, t.shape, 0)
    down = jnp.where(row == 0, prev_row, pltpu.roll(t, 1, 0))
    up = jnp.where(row == n - 1, next_row, pltpu.roll(t, n - 1, 0))
    return down, up


def _merge_kernel(x_ref, mod_ref, g1_ref, of_ref, ob_ref, oat_ref, u_ref, up_ref, un_ref,
                  wrsg_ref, glag_ref, wbra_ref, wbrb_ref, wbrc_ref, scw_ref, wo_ref, xo_ref):
    i = pl.program_id(1)
    x = x_ref[0]
    m = mod_ref[0]
    d = x.shape[1]
    hb = (_rms(x, g1_ref[...]) * (1.0 + m[1:2]) + m[0:1]).astype(BF)
    rsg = _dot(hb, wrsg_ref[...])
    r_a = rsg[:, :V_W]
    sb = rsg[:, V_W:V_W + SC_W]
    gates = jax.nn.sigmoid(rsg[:, V_W + SC_W:])

    o = of_ref[0] + ob_ref[0]
    gg = glag_ref[...]
    on = jnp.concatenate(
        [_rms(o[:, GLA_DV * hh:GLA_DV * (hh + 1)], gg[:, GLA_DV * hh:GLA_DV * (hh + 1)])
         for hh in range(GLA_HEADS)], axis=1)
    y_a = _dot((on * _silu(r_a)).astype(BF), wbra_ref[...])
    y_b = _dot(oat_ref[0], wbrb_ref[...])

    u = u_ref[0]
    prev_row = jnp.where(i > 0, up_ref[0, HALO - 1:HALO, :], 0.0)
    next_row = jnp.where(i < pl.num_programs(1) - 1, un_ref[0, 0:1, :], 0.0)
    u_dn, u_up = _shift_rows(u, prev_row, next_row)
    w = scw_ref[...]
    conv = u_dn * w[0:1] + u * w[1:2] + u_up * w[2:3]
    y_c = _dot((sb * conv).astype(BF), wbrc_ref[...])

    mix = gates[:, :d] * y_a + gates[:, d:2 * d] * y_b + gates[:, 2 * d:] * y_c
    xo_ref[0] = x + m[2:3] * _dot(mix.astype(BF), wo_ref[...])


def _halo_specs(tm, l, c):
    nb8 = tm // HALO
    last8 = l // HALO - 1
    prev = pl.BlockSpec((1, HALO, c), lambda bi, i: (bi, jnp.maximum(i * nb8 - 1, 0), 0))
    nxt = pl.BlockSpec((1, HALO, c), lambda bi, i: (bi, jnp.minimum((i + 1) * nb8, last8), 0))
    return prev, nxt


def _merge_call(x, mod, o_f, o_b, o_att, u, lw, tm):
    b, l, d = x.shape
    tok = lambda c: pl.BlockSpec((1, tm, c), lambda bi, i: (bi, i, 0))
    up_spec, un_spec = _halo_specs(tm, l, SC_W)
    weights = [lw["wrsg"], lw["glag"], lw["wbra"], lw["wbrb"], lw["wbrc"], lw["scw"], lw["wo"]]
    return pl.pallas_call(
        _merge_kernel,
        grid=(b, l // tm),
        in_specs=[tok(d), pl.BlockSpec((1, 6, d), lambda bi, i: (bi, 0, 0)), _const_spec(lw["g1"].shape),
                  tok(V_W), tok(V_W), tok(MLA_HEADS * MLA_DV), tok(SC_W), up_spec, un_spec]
                 + [_const_spec(w.shape) for w in weights],
        out_specs=tok(d),
        out_shape=jax.ShapeDtypeStruct((b, l, d), F32),
        compiler_params=_params(("parallel", "parallel")),
        name="merge",
    )(x, mod, lw["g1"], o_f, o_b, o_att, u, u, u, *weights)


def _ffn_kernel(x_ref, xp_ref, xn_ref, mod_ref, g2_ref, wg_ref, wu_ref, cw_ref, wd_ref, gf_ref,
                xo_ref, gs_ref, a_ref, *, fc, final):
    i = pl.program_id(1)
    x = x_ref[0]
    m = mod_ref[0]
    tm = x.shape[0]
    nf = wg_ref.shape[1]

    def hmod(t):
        return _rms(t, g2_ref[...]) * (1.0 + m[4:5]) + m[3:4]

    h = hmod(x)
    hb = h.astype(BF)
    h_prev = jnp.where(i > 0, hmod(xp_ref[0]), 0.0)
    h_next = jnp.where(i < pl.num_programs(1) - 1, hmod(xn_ref[0]), 0.0)
    h_ext = jnp.concatenate([h_prev, h, h_next], axis=0).astype(BF)
    nchunks = nf // fc

    def gate_up(ci):
        cols = slice(ci * fc, (ci + 1) * fc)
        return _dot(h_ext, wg_ref[:, cols]), _dot(hb, wu_ref[:, cols])

    ahead = gate_up(0)
    for ci in range(nchunks):
        cols = slice(ci * fc, (ci + 1) * fc)
        g_ext, up = ahead
        if ci + 1 < nchunks:
            ahead = gate_up(ci + 1)
        cw = cw_ref[:, cols]
        gs = gs_ref.at[ci % 2]
        gs[...] = g_ext
        g = (gs[HALO - 1:HALO - 1 + tm, :] * cw[0:1] + gs[HALO:HALO + tm, :] * cw[1:2]
             + gs[HALO + 1:HALO + 1 + tm, :] * cw[2:3] + cw[3:4])
        a_ref[:, cols] = (_silu(g) * up).astype(BF)
    acc = _dot(a_ref[...], wd_ref[...])
    y = x + m[5:6] * acc
    if final:
        y = _rms(y, gf_ref[...])
    xo_ref[0] = y


def _ffn_call(x, mod, lw, gfinal, tm, final):
    b, l, d = x.shape
    nf = lw["wg"].shape[1]
    fc = 256
    tok = pl.BlockSpec((1, tm, d), lambda bi, i: (bi, i, 0))
    xp_spec, xn_spec = _halo_specs(tm, l, d)
    weights = [lw["g2"], lw["wg"], lw["wu"], lw["cw"], lw["wd"], gfinal]
    return pl.pallas_call(
        functools.partial(_ffn_kernel, fc=fc, final=final),
        grid=(b, l // tm),
        in_specs=[tok, xp_spec, xn_spec, pl.BlockSpec((1, 6, d), lambda bi, i: (bi, 0, 0))]
                 + [_const_spec(w.shape) for w in weights],
        out_specs=tok,
        out_shape=jax.ShapeDtypeStruct((b, l, d), F32),
        scratch_shapes=[pltpu.VMEM((2, tm + 2 * HALO, fc), F32), pltpu.VMEM((tm, nf), BF)],
        compiler_params=_params(("parallel", "parallel")),
        name="ffn",
    )(x, x, x, mod, *weights)


def _rope_perm():
    idx = np.zeros((MLA_ROPE,), np.int32)
    sgn = np.zeros((MLA_ROPE,), np.float32)
    for ax in range(2):
        for f in range(ROPE_FREQS):
            lo = ax * 2 * ROPE_FREQS + f
            hi = lo + ROPE_FREQS
            idx[lo], sgn[lo] = hi, -1.0
            idx[hi], sgn[hi] = lo, 1.0
    return idx, sgn


def _rope_tables(l, rotate):
    if rotate:
        rows = l // GRID_W
        row = jnp.repeat(jnp.arange(rows, dtype=F32), GRID_W)
        col = jnp.tile(jnp.arange(GRID_W, dtype=F32), rows)
        inv = ROPE_THETA ** (-jnp.arange(ROPE_FREQS, dtype=F32) / ROPE_FREQS)
        ang = jnp.stack([row[:, None] * inv, col[:, None] * inv], axis=1)
        cos, sin = jnp.cos(ang), jnp.sin(ang)
        expand = lambda t: jnp.broadcast_to(t[:, :, None, :], (l, 2, 2, ROPE_FREQS)).reshape(l, MLA_ROPE)
        cos32, sin32 = expand(cos), expand(sin)
    else:
        cos32, sin32 = jnp.ones((l, MLA_ROPE), F32), jnp.zeros((l, MLA_ROPE), F32)
    cosq = jnp.tile(cos32, (1, MLA_HEADS)).T
    sinq = jnp.tile(sin32, (1, MLA_HEADS)).T
    pad = lambda t: jnp.pad(t, ((0, 0), (0, 128 - MLA_ROPE)))
    return cosq, sinq, pad(cos32), pad(sin32)


def _layer_weights(l, p):
    d = p["w_in"].shape[1]
    w_in = p["w_in"][l]
    sizes = (QK_W, QK_W, V_W, V_W, 2 * GLA_RANK, MLA_Q_LORA, MLA_KV_LORA, MLA_ROPE, SC_W, SC_W, SC_W,
             N_BRANCH * d)
    offs = np.concatenate([[0], np.cumsum(sizes)])
    col = lambda a, b: w_in[:, int(offs[a]):int(offs[b])]
    idx, sgn = _rope_perm()
    w_kr = col(7, 8)
    w_kr_rot = w_kr[:, idx] * sgn
    zeros = lambda n: jnp.zeros((d, n), F32)
    wgk = jnp.zeros((128, 2 * QK_W), F32)
    wgk = wgk.at[MLA_ROPE:MLA_ROPE + GLA_RANK, :QK_W].set(p["w_gk2"][l, 0])
    wgk = wgk.at[MLA_ROPE + GLA_RANK:MLA_ROPE + 2 * GLA_RANK, QK_W:].set(p["w_gk2"][l, 1])
    w_uq = p["w_uq"][l].reshape(MLA_Q_LORA, MLA_HEADS, MLA_NOPE + MLA_ROPE)
    w_uq_rope = w_uq[:, :, MLA_NOPE:]
    w_ukv = p["w_ukv"][l].reshape(MLA_KV_LORA, MLA_HEADS, MLA_NOPE + MLA_DV)
    nf = p["w_ffn_gate"].shape[2]
    cw = jnp.concatenate([p["ffn_conv_w"][l], p["ffn_conv_b"][l][None], jnp.zeros((4, nf), F32)], axis=0)
    scw = jnp.concatenate([p["sc_w"][l], jnp.zeros((5, SC_W), F32)], axis=0)
    bf = lambda t: t.astype(BF)
    return {
        "g1": p["norm1_g"][l][None],
        "wqkv": bf(col(0, 3)),
        "wsm1": bf(jnp.concatenate([w_kr, col(4, 5), zeros(128 - MLA_ROPE - 2 * GLA_RANK)], axis=1)),
        "wsm2": bf(jnp.concatenate([w_kr_rot, zeros(128 - MLA_ROPE)], axis=1)),
        "wgk": bf(wgk),
        "bgk": p["b_gk2"][l].reshape(1, 2 * QK_W),
        "wcq": bf(col(5, 6)),
        "gq": p["mla_q_norm_g"][l][None],
        "wuqn": bf(w_uq[:, :, :MLA_NOPE].reshape(MLA_Q_LORA, -1).T),
        "wuk": bf(jnp.transpose(w_ukv[:, :, :MLA_NOPE], (1, 0, 2))),
        "wuqr": bf(w_uq_rope.reshape(MLA_Q_LORA, -1).T),
        "wuqrr": bf((w_uq_rope[:, :, idx] * sgn).reshape(MLA_Q_LORA, -1).T),
        "wckv": bf(col(6, 7)),
        "gkv": p["mla_kv_norm_g"][l][None],
        "ws": bf(col(9, 11)),
        "wuv": bf(jnp.transpose(w_ukv[:, :, MLA_NOPE:], (1, 2, 0))),
        "wrsg": bf(jnp.concatenate([col(3, 4), col(8, 9), col(11, 12)], axis=1)),
        "glag": p["gla_norm_g"][l][None],
        "wbra": bf(p["w_br_a"][l]),
        "wbrb": bf(p["w_br_b"][l]),
        "wbrc": bf(p["w_br_c"][l]),
        "scw": scw,
        "wo": bf(p["w_o"][l]),
        "g2": p["norm2_g"][l][None],
        "wg": bf(p["w_ffn_gate"][l]),
        "wu": bf(p["w_ffn_up"][l]),
        "cw": cw,
        "wd": bf(p["w_ffn_down"][l]),
    }


def _pick(n, options):
    for o in options:
        if n % o == 0:
            return o
    raise ValueError(f"no tile in {options} divides {n}")


def kernel(x, c, ctx, c_ctx, w_ada, b_ada, norm1_g, w_in, w_gk2, b_gk2, gla_norm_g, mla_q_norm_g, w_uq,
           mla_kv_norm_g, w_ukv, sc_w, w_br_a, w_br_b, w_br_c, w_o, norm2_g, w_ffn_gate, w_ffn_up,
           ffn_conv_w, ffn_conv_b, w_ffn_down, final_norm_g):
    p = dict(w_in=w_in, w_gk2=w_gk2, b_gk2=b_gk2, gla_norm_g=gla_norm_g, mla_q_norm_g=mla_q_norm_g,
             w_uq=w_uq, mla_kv_norm_g=mla_kv_norm_g, w_ukv=w_ukv, sc_w=sc_w, w_br_a=w_br_a,
             w_br_b=w_br_b, w_br_c=w_br_c, w_o=w_o, norm1_g=norm1_g, norm2_g=norm2_g,
             w_ffn_gate=w_ffn_gate, w_ffn_up=w_ffn_up, ffn_conv_w=ffn_conv_w, ffn_conv_b=ffn_conv_b,
             w_ffn_down=w_ffn_down)
    b, l, d = x.shape
    lc = ctx.shape[1]
    depth = w_in.shape[0]
    assert l % PAIR == 0 and lc % PAIR == 0 and l % GRID_W == 0

    tiles = (512, 256, 128)
    tm, tm_c = _pick(l, tiles), _pick(lc, tiles)
    tb, tb_c = _pick(l, (1024,) + tiles), _pick(lc, tiles)
    tq, tq_c = _pick(l, tiles), _pick(lc, tiles)
    key_tiles = (3 * KEY_SUB, 2 * KEY_SUB, KEY_SUB)
    tk, tk_c = _pick(l + lc, key_tiles), _pick(lc, key_tiles)

    rows = -(-(b + 1) // 8) * 8
    cc = jnp.concatenate([c, c_ctx[None], jnp.zeros((rows - b - 1, d), F32)], axis=0)
    mod_all = _mod_call(cc, w_ada, b_ada)
    tabs = _rope_tables(l, True)
    tabs_c = _rope_tables(lc, False)
    gfinal = final_norm_g[None]
    zero_state = jnp.zeros((b, 2, V_W, QK_W), F32)

    xc = ctx
    for li in range(depth):
        last = li == depth - 1
        lw = _layer_weights(li, p)
        mod = mod_all[li, :b].reshape(b, 6, d)
        mod_c = jnp.broadcast_to(mod_all[li, b].reshape(1, 6, d), (b, 6, d))

        qk_c, v_c, vt_c, la_c, qt_c, kabs_c, ct_c, u_c = _proj_call(xc, mod_c, lw, tabs_c, tm_c)
        of_c, ob_c, s_ctx = _gla_call(qk_c, v_c, vt_c, la_c, zero_state, tb_c)

        qk, v, vt, la, qt, kabs, ct, u = _proj_call(x, mod, lw, tabs, tm)
        o_f, o_b, _ = _gla_call(qk, v, vt, la, s_ctx, tb)
        k_all = jnp.concatenate([kabs, kabs_c], axis=1)
        ct_all = jnp.concatenate([ct, ct_c], axis=2)
        o_att = _attn_call(qt, k_all, ct_all, lw["wuv"], tq, tk)
        x = _merge_call(x, mod, o_f, o_b, o_att, u, lw, tm)
        if not last:
            o_att_c = _attn_call(qt_c, kabs_c, ct_c, lw["wuv"], tq_c, tk_c)
            xc = _merge_call(xc, mod_c, of_c, ob_c, o_att_c, u_c, lw, tm_c)
        x = _ffn_call(x, mod, lw, gfinal, tm, last)
        if not last:
            xc = _ffn_call(xc, mod_c, lw, gfinal, tm_c, False)
    return x
```

```python
import functools

import numpy as np
import jax
import jax.numpy as jnp
from jax import lax
from jax.experimental import pallas as pl
from jax.experimental.pallas import tpu as pltpu

F32 = jnp.float32
BF = jnp.bfloat16

GRID_W = 64
GLA_HEADS = 4
GLA_DK = 64
GLA_DV = 128
GLA_RANK = 16
GLA_GATE_NORM = 16.0
GLA_CHUNK = 64
MLA_HEADS = 8
MLA_Q_LORA = 256
MLA_KV_LORA = 128
MLA_NOPE = 64
MLA_ROPE = 32
MLA_DV = 64
ROPE_THETA = 10000.0
ROPE_FREQS = MLA_ROPE // 4
SC_W = 512
NORM_EPS = 1e-6
N_BRANCH = 3

QK_W = GLA_HEADS * GLA_DK
V_W = GLA_HEADS * GLA_DV
Q_ABS = MLA_KV_LORA + MLA_ROPE
PAIR = 2 * GLA_CHUNK
HALO = 8
NEG_BIG = -1e30
ONES_ROWS = 16
V_ROWS = MLA_DV + ONES_ROWS
LOG2E = 1.4426950408889634
KEY_SUB = 256
ATTN_SKEW = 2
SCORE_SLOTS = 4
assert MLA_HEADS % SCORE_SLOTS == 0 and SCORE_SLOTS > ATTN_SKEW

VMEM_LIMIT = 56 * 1024 * 1024


def _dot(a, b):
    return jnp.dot(a, b, preferred_element_type=F32)


def _dot_nt(a, b):
    return lax.dot_general(a, b, (((1,), (1,)), ((), ())), preferred_element_type=F32)


def _rms(x, g):
    return x * lax.rsqrt(jnp.mean(x * x, axis=-1, keepdims=True) + NORM_EPS) * g


def _silu(x):
    return x * jax.nn.sigmoid(x)


def _const_spec(shape):
    nd = len(shape)
    return pl.BlockSpec(shape, lambda *_: (0,) * nd, pipeline_mode=pl.Buffered(1))


def _params(sem):
    return pltpu.CompilerParams(dimension_semantics=sem, vmem_limit_bytes=VMEM_LIMIT)


def _mod_kernel(c_ref, w_ref, b_ref, o_ref):
    a = _silu(c_ref[...]).astype(BF)
    o_ref[0] = _dot(a, w_ref[0].astype(BF)) + b_ref[0]


def _mod_call(cc, w_ada, b_ada):
    nl, d, n = w_ada.shape
    rows = cc.shape[0]
    tn = 1024
    return pl.pallas_call(
        _mod_kernel,
        grid=(nl, n // tn),
        in_specs=[
            pl.BlockSpec((rows, d), lambda l, j: (0, 0)),
            pl.BlockSpec((1, d, tn), lambda l, j: (l, 0, j)),
            pl.BlockSpec((1, 1, tn), lambda l, j: (l, 0, j)),
        ],
        out_specs=pl.BlockSpec((1, rows, tn), lambda l, j: (l, 0, j)),
        out_shape=jax.ShapeDtypeStruct((nl, rows, n), F32),
        compiler_params=_params(("parallel", "parallel")),
        name="adaln_mod",
    )(cc, w_ada, b_ada.reshape(nl, 1, n))


def _proj_kernel(x_ref, mod_ref, g1_ref, wqkv_ref, wsm1_ref, wsm2_ref, wgk_ref, bgk_ref,
                 wcq_ref, gq_ref, wuqn_ref, wuk_ref, wuqr_ref, wuqrr_ref, wckv_ref, gkv_ref, wuvt_ref,
                 ws_ref, cosq_ref, sinq_ref, cosk_ref, sink_ref,
                 qk_ref, v_ref, vt_ref, la_ref, qt_ref, kabs_ref, vt_mla_ref, u_ref):
    x = x_ref[0]
    m = mod_ref[0]
    h = _rms(x, g1_ref[...]) * (1.0 + m[1:2]) + m[0:1]
    hb = h.astype(BF)

    sm1 = _dot(hb, wsm1_ref[...])
    sm2 = _dot(hb, wsm2_ref[...])
    cq = _dot(hb, wcq_ref[...])
    ckv = _dot(hb, wckv_ref[...])

    qkv = _dot(hb, wqkv_ref[...])

    z = _dot(sm1.astype(BF), wgk_ref[...]) + bgk_ref[...]
    la_ref[0] = jax.nn.log_sigmoid(z) * (1.0 / GLA_GATE_NORM)

    scale = (MLA_NOPE + MLA_ROPE) ** -0.5 * LOG2E
    cqn = _rms(cq, gq_ref[...]).astype(BF)
    qnt = _dot_nt(wuqn_ref[...], cqn).astype(BF)
    qr_t = _dot_nt(wuqr_ref[...], cqn)
    qrr_t = _dot_nt(wuqrr_ref[...], cqn)

    s = _dot(hb, ws_ref[...])

    qk_ref[0] = qkv[:, :2 * QK_W].astype(BF)
    v = qkv[:, 2 * QK_W:]
    v_ref[0] = v.astype(BF)
    vt_ref[0] = v.T.astype(BF)

    qrope_t = (qr_t * cosq_ref[...] + qrr_t * sinq_ref[...]) * scale
    for hh in range(MLA_HEADS):
        qa = _dot(wuk_ref[hh], qnt[MLA_NOPE * hh:MLA_NOPE * (hh + 1), :]) * scale
        qt_ref[0, Q_ABS * hh:Q_ABS * hh + MLA_KV_LORA, :] = qa.astype(BF)
        qt_ref[0, Q_ABS * hh + MLA_KV_LORA:Q_ABS * (hh + 1), :] = (
            qrope_t[MLA_ROPE * hh:MLA_ROPE * (hh + 1), :].astype(BF))

    cn = _rms(ckv, gkv_ref[...]).astype(BF)
    kabs_ref[0, :, 0:MLA_KV_LORA] = cn
    kr = sm1 * cosk_ref[...] + sm2 * sink_ref[...]
    kabs_ref[0, :, MLA_KV_LORA:Q_ABS] = kr[:, 0:MLA_ROPE].astype(BF)
    val_t = _dot_nt(wuvt_ref[...], cn).astype(BF)
    ones = jnp.ones((ONES_ROWS, val_t.shape[1]), BF)
    for hh in range(MLA_HEADS):
        vt_mla_ref[0, V_ROWS * hh:V_ROWS * hh + MLA_DV, :] = val_t[MLA_DV * hh:MLA_DV * (hh + 1), :]
        vt_mla_ref[0, V_ROWS * hh + MLA_DV:V_ROWS * (hh + 1), :] = ones

    u_ref[0] = s[:, :SC_W] * s[:, SC_W:]


def _proj_call(x, mod, lw, tabs, tm):
    b, l, d = x.shape
    cosq, sinq, cosk, sink = tabs
    tok = lambda c: pl.BlockSpec((1, tm, c), lambda bi, i: (bi, i, 0))
    tr = lambda r: pl.BlockSpec((1, r, tm), lambda bi, i: (bi, 0, i))
    weights = [lw["g1"], lw["wqkv"], lw["wsm1"], lw["wsm2"], lw["wgk"], lw["bgk"], lw["wcq"],
               lw["gq"], lw["wuqn"], lw["wuk"], lw["wuqr"], lw["wuqrr"], lw["wckv"], lw["gkv"],
               lw["wuvt"], lw["ws"]]
    in_specs = ([tok(d), pl.BlockSpec((1, 6, d), lambda bi, i: (bi, 0, 0))]
                + [_const_spec(w.shape) for w in weights]
                + [pl.BlockSpec((MLA_HEADS * MLA_ROPE, tm), lambda bi, i: (0, i)),
                   pl.BlockSpec((MLA_HEADS * MLA_ROPE, tm), lambda bi, i: (0, i)),
                   pl.BlockSpec((tm, 128), lambda bi, i: (i, 0)),
                   pl.BlockSpec((tm, 128), lambda bi, i: (i, 0))])
    out_shape = (
        jax.ShapeDtypeStruct((b, l, 2 * QK_W), BF),
        jax.ShapeDtypeStruct((b, l, V_W), BF),
        jax.ShapeDtypeStruct((b, V_W, l), BF),
        jax.ShapeDtypeStruct((b, l, 2 * QK_W), F32),
        jax.ShapeDtypeStruct((b, MLA_HEADS * Q_ABS, l), BF),
        jax.ShapeDtypeStruct((b, l, Q_ABS), BF),
        jax.ShapeDtypeStruct((b, MLA_HEADS * V_ROWS, l), BF),
        jax.ShapeDtypeStruct((b, l, SC_W), F32),
    )
    out_specs = (tok(2 * QK_W), tok(V_W), tr(V_W), tok(2 * QK_W), tr(MLA_HEADS * Q_ABS),
                 tok(Q_ABS), tr(MLA_HEADS * V_ROWS), tok(SC_W))
    return pl.pallas_call(
        _proj_kernel,
        grid=(b, l // tm),
        in_specs=in_specs,
        out_specs=out_specs,
        out_shape=out_shape,
        compiler_params=_params(("parallel", "parallel")),
        name="proj",
    )(x, mod, *weights, cosq, sinq, cosk, sink)


def _gla_kernel(qkf_ref, qkb_ref, vf_ref, vb_ref, vtf_ref, vtb_ref, laf_ref, lab_ref, s0_ref,
                of_ref, ob_ref, sfin_ref, s_ref, *, tb):
    n = pl.program_id(1)
    npairs = tb // PAIR

    @pl.when(n == 0)
    def _():
        s_ref[...] = s0_ref[0]

    r = lax.broadcasted_iota(jnp.int32, (PAIR, PAIR), 0)
    c = lax.broadcasted_iota(jnp.int32, (PAIR, PAIR), 1)
    same = (r // GLA_CHUNK) == (c // GLA_CHUNK)
    ones_c = jnp.where(same, 1.0, 0.0).astype(BF)
    lane_head = lax.broadcasted_iota(jnp.int32, (PAIR, QK_W), 1) // GLA_DK
    row_chunk = lax.broadcasted_iota(jnp.int32, (PAIR, QK_W), 0) // GLA_CHUNK
    bd_mask = (lax.broadcasted_iota(jnp.int32, (V_W, QK_W), 0) // GLA_DV
               == lax.broadcasted_iota(jnp.int32, (V_W, QK_W), 1) // GLA_DK)

    def stage_decay(d):
        tri_b = jnp.where(d["tri"], 1.0, 0.0).astype(BF)
        la = d["la_ref"][0, d["rows"], :]
        hi = la.astype(BF)
        lo = (la - hi.astype(F32)).astype(BF)
        d["bcum"] = _dot(tri_b, hi) + _dot(tri_b, lo)
        d["blast"] = _dot(ones_c, hi) + _dot(ones_c, lo)

    def stage_scale(d):
        qk = d["qk_ref"][0, d["rows"], :].astype(F32)
        q = qk[:, :QK_W] * (GLA_DK ** -0.5)
        k = qk[:, QK_W:]
        late = row_chunk == (1 if d["forward"] else 0)
        blast = d["blast"]
        other = jnp.concatenate([blast[GLA_CHUNK:], blast[:GLA_CHUNK]], axis=0)
        d_other = jnp.exp(other)
        q_in = q * jnp.exp(d["bcum"])
        k_st = k * jnp.exp(blast - d["bcum"])
        d["q_in"] = q_in.astype(BF)
        d["k_in"] = (k * jnp.exp(-d["bcum"])).astype(BF)
        d["k_st"] = k_st.astype(BF)
        d["q_state"] = jnp.where(late, q_in * d_other, q_in).astype(BF)
        d["k_state"] = jnp.where(late, k_st, k_st * d_other).astype(BF)
        d["decay"] = jnp.exp(blast[0:1] + other[0:1])

    def stage_state_inc(d):
        vt = d["vt_ref"][0, :, d["rows"]]
        d["ut"] = jnp.where(bd_mask, _dot(vt, d["k_state"]), 0.0)

    def stage_scores(d):
        q_in = d["q_in"]
        qs = jnp.concatenate(
            [jnp.where(lane_head == hh, q_in, jnp.zeros_like(q_in)) for hh in range(GLA_HEADS)], axis=0)
        a2 = _dot_nt(qs, jnp.concatenate([d["k_in"], d["k_st"]], axis=0))
        cross = ((r >= GLA_CHUNK) & (c < GLA_CHUNK)) if d["forward"] else ((r < GLA_CHUNK) & (c >= GLA_CHUNK))
        tri4 = jnp.concatenate([d["tri"]] * GLA_HEADS, axis=0)
        cross4 = jnp.concatenate([cross] * GLA_HEADS, axis=0)
        d["a"] = jnp.where(tri4, a2[:, :PAIR], jnp.where(cross4, a2[:, PAIR:], 0.0)).astype(BF)

    def stage_intra(d):
        v = d["v_ref"][0, d["rows"], :]
        a = d["a"]
        d["o_intra"] = jnp.concatenate(
            [_dot(a[PAIR * hh:PAIR * (hh + 1)], v[:, GLA_DV * hh:GLA_DV * (hh + 1)])
             for hh in range(GLA_HEADS)], axis=1)

    def stage_scan(d):
        st = s_ref[d["sidx"]]
        d["o_ref"][0, d["rows"], :] = d["o_intra"] + _dot_nt(d["q_state"], st.astype(BF))
        s_ref[d["sidx"]] = st * d["decay"] + d["ut"]

    def body(p, carry):
        dirs = [dict(qk_ref=qkf_ref, v_ref=vf_ref, vt_ref=vtf_ref, la_ref=laf_ref, o_ref=of_ref, sidx=0,
                     forward=True, start=pl.multiple_of(p * PAIR, PAIR)),
                dict(qk_ref=qkb_ref, v_ref=vb_ref, vt_ref=vtb_ref, la_ref=lab_ref, o_ref=ob_ref, sidx=1,
                     forward=False, start=pl.multiple_of((npairs - 1 - p) * PAIR, PAIR))]
        for d in dirs:
            d["rows"] = pl.ds(d["start"], PAIR)
            d["tri"] = jnp.logical_and(same, (c <= r) if d["forward"] else (c >= r))
        for stage in (stage_decay, stage_scale, stage_state_inc, stage_scores, stage_intra, stage_scan):
            for d in dirs:
                stage(d)
        return carry

    lax.fori_loop(0, npairs, body, 0)

    @pl.when(n == pl.num_programs(1) - 1)
    def _():
        sfin_ref[0] = s_ref[...]


def _gla_call(qk, v, vt, la, s0, tb):
    b, l, _ = qk.shape
    nb = l // tb
    fwd = lambda c: pl.BlockSpec((1, tb, c), lambda bi, i: (bi, i, 0))
    bwd = lambda c: pl.BlockSpec((1, tb, c), lambda bi, i: (bi, nb - 1 - i, 0))
    st_spec = pl.BlockSpec((1, 2, V_W, QK_W), lambda bi, i: (bi, 0, 0, 0))
    return pl.pallas_call(
        functools.partial(_gla_kernel, tb=tb),
        grid=(b, nb),
        in_specs=[fwd(2 * QK_W), bwd(2 * QK_W), fwd(V_W), bwd(V_W),
                  pl.BlockSpec((1, V_W, tb), lambda bi, i: (bi, 0, i)),
                  pl.BlockSpec((1, V_W, tb), lambda bi, i: (bi, 0, nb - 1 - i)),
                  pl.BlockSpec((1, tb, QK_W), lambda bi, i: (bi, i, 0)),
                  pl.BlockSpec((1, tb, QK_W), lambda bi, i: (bi, nb - 1 - i, 1)),
                  st_spec],
        out_specs=(fwd(V_W), bwd(V_W), st_spec),
        out_shape=(jax.ShapeDtypeStruct((b, l, V_W), F32),
                   jax.ShapeDtypeStruct((b, l, V_W), F32),
                   jax.ShapeDtypeStruct((b, 2, V_W, QK_W), F32)),
        scratch_shapes=[pltpu.VMEM((2, V_W, QK_W), F32)],
        compiler_params=_params(("parallel", "arbitrary")),
        name="gla",
    )(qk, qk, v, v, vt, vt, la, la, s0)


def _attn_kernel(qt_ref, k_ref, vt_ref, o_ref, m_ref, acc_ref, ot_ref, s_ref, p_ref, *, tk):
    tq = qt_ref.shape[2]
    nchunks = k_ref.shape[1] // tk
    nsub = tk // KEY_SUB
    m_ref[...] = jnp.full(m_ref.shape, NEG_BIG, F32)
    acc_ref[...] = jnp.zeros(acc_ref.shape, F32)

    def score_sub(kstart, hh, i, slot, mpart):
        ks = k_ref[0, pl.ds(kstart + KEY_SUB * i, KEY_SUB), :]
        s = _dot(ks, qt_ref[0, Q_ABS * hh:Q_ABS * (hh + 1), :])
        s_ref[slot, KEY_SUB * i:KEY_SUB * (i + 1), :] = s
        return jnp.maximum(mpart, jnp.max(s.reshape(KEY_SUB // 8, 8, tq), axis=0))

    def score_finish(hh, mpart):
        m_old = m_ref[hh:hh + 1, :]
        m_new = jnp.maximum(m_old, jnp.max(mpart, axis=0, keepdims=True))
        m_ref[hh:hh + 1, :] = m_new
        return m_new, jnp.exp2(m_old - m_new)

    def value_sub(i, slot, pslot, m_new):
        rows = slice(KEY_SUB * i, KEY_SUB * (i + 1))
        p_ref[pslot, rows, :] = jnp.exp2(s_ref[slot, rows, :] - m_new).astype(BF)

    mpart0 = jnp.full((8, tq), NEG_BIG, F32)
    ahead = []
    for hh in range(ATTN_SKEW):
        mp = mpart0
        for i in range(nsub):
            mp = score_sub(0, hh, i, hh % SCORE_SLOTS, mp)
        ahead.extend(score_finish(hh, mp))

    def chunk_body(j, carry):
        pend = [(carry[2 * u], carry[2 * u + 1]) for u in range(ATTN_SKEW)]
        start = pl.multiple_of(j * tk, KEY_SUB)
        start_next = pl.multiple_of(jnp.minimum(j + 1, nchunks - 1) * tk, KEY_SUB)
        for hh in range(MLA_HEADS):
            m_cur, alpha_cur = pend.pop(0)
            nh = (hh + ATTN_SKEW) % MLA_HEADS
            nstart = start if hh + ATTN_SKEW < MLA_HEADS else start_next
            mp = mpart0
            for i in range(nsub):
                value_sub(i, hh % SCORE_SLOTS, hh % 2, m_cur)
                mp = score_sub(nstart, nh, i, (hh + ATTN_SKEW) % SCORE_SLOTS, mp)
            vals = vt_ref[0, V_ROWS * hh:V_ROWS * (hh + 1), pl.ds(start, tk)]
            acc_ref[hh] = alpha_cur * acc_ref[hh] + _dot(vals, p_ref[hh % 2])
            pend.append(score_finish(nh, mp))
        return tuple(v for pair in pend for v in pair)

    lax.fori_loop(0, nchunks, chunk_body, tuple(ahead))
    for hh in range(MLA_HEADS):
        a = acc_ref[hh]
        ot_ref[MLA_DV * hh:MLA_DV * (hh + 1), :] = a[:MLA_DV] / a[MLA_DV:MLA_DV + 1]
    o_ref[0] = ot_ref[...].T.astype(BF)


def _attn_call(qt, kabs, vt, tq, tk):
    b, _, l = qt.shape
    lk = kabs.shape[1]
    return pl.pallas_call(
        functools.partial(_attn_kernel, tk=tk),
        grid=(b, l // tq),
        in_specs=[pl.BlockSpec((1, MLA_HEADS * Q_ABS, tq), lambda bi, i: (bi, 0, i)),
                  pl.BlockSpec((1, lk, Q_ABS), lambda bi, i: (bi, 0, 0), pipeline_mode=pl.Buffered(1)),
                  pl.BlockSpec((1, MLA_HEADS * V_ROWS, lk), lambda bi, i: (bi, 0, 0),
                               pipeline_mode=pl.Buffered(1))],
        out_specs=pl.BlockSpec((1, tq, MLA_HEADS * MLA_DV), lambda bi, i: (bi, i, 0)),
        out_shape=jax.ShapeDtypeStruct((b, l, MLA_HEADS * MLA_DV), BF),
        scratch_shapes=[pltpu.VMEM((MLA_HEADS, tq), F32),
                        pltpu.VMEM((MLA_HEADS, V_ROWS, tq), F32),
                        pltpu.VMEM((MLA_HEADS * MLA_DV, tq), F32),
                        pltpu.VMEM((SCORE_SLOTS, tk, tq), F32),
                        pltpu.VMEM((2, tk, tq), BF)],
        compiler_params=_params(("parallel", "parallel")),
        name="attn",
    )(qt, kabs, vt)


def _shift_rows(t, prev_row, next_row):
    n = t.shape[0]
    row = lax.broadcasted_iota(jnp.int32, t.shape, 0)
    down = jnp.where(row == 0, prev_row, pltpu.roll(t, 1, 0))
    up = jnp.where(row == n - 1, next_row, pltpu.roll(t, n - 1, 0))
    return down, up


def _merge_kernel(x_ref, mod_ref, g1_ref, of_ref, ob_ref, oat_ref, u_ref, up_ref, un_ref,
                  wrsg_ref, glag_ref, wbra_ref, wbrb_ref, wbrc_ref, scw_ref, wo_ref, xo_ref):
    i = pl.program_id(1)
    x = x_ref[0]
    m = mod_ref[0]
    d = x.shape[1]
    hb = (_rms(x, g1_ref[...]) * (1.0 + m[1:2]) + m[0:1]).astype(BF)
    rsg = _dot(hb, wrsg_ref[...])
    r_a = rsg[:, :V_W]
    sb = rsg[:, V_W:V_W + SC_W]
    gates = jax.nn.sigmoid(rsg[:, V_W + SC_W:])

    o = of_ref[0] + ob_ref[0]
    gg = glag_ref[...]
    on = jnp.concatenate(
        [_rms(o[:, GLA_DV * hh:GLA_DV * (hh + 1)], gg[:, GLA_DV * hh:GLA_DV * (hh + 1)])
         for hh in range(GLA_HEADS)], axis=1)
    y_a = _dot((on * _silu(r_a)).astype(BF), wbra_ref[...])
    y_b = _dot(oat_ref[0], wbrb_ref[...])

    u = u_ref[0]
    prev_row = jnp.where(i > 0, up_ref[0, HALO - 1:HALO, :], 0.0)
    next_row = jnp.where(i < pl.num_programs(1) - 1, un_ref[0, 0:1, :], 0.0)
    u_dn, u_up = _shift_rows(u, prev_row, next_row)
    w = scw_ref[...]
    conv = u_dn * w[0:1] + u * w[1:2] + u_up * w[2:3]
    y_c = _dot((sb * conv).astype(BF), wbrc_ref[...])

    mix = gates[:, :d] * y_a + gates[:, d:2 * d] * y_b + gates[:, 2 * d:] * y_c
    xo_ref[0] = x + m[2:3] * _dot(mix.astype(BF), wo_ref[...])


def _halo_specs(tm, l, c):
    nb8 = tm // HALO
    last8 = l // HALO - 1
    prev = pl.BlockSpec((1, HALO, c), lambda bi, i: (bi, jnp.maximum(i * nb8 - 1, 0), 0))
    nxt = pl.BlockSpec((1, HALO, c), lambda bi, i: (bi, jnp.minimum((i + 1) * nb8, last8), 0))
    return prev, nxt


def _merge_call(x, mod, o_f, o_b, o_att, u, lw, tm):
    b, l, d = x.shape
    tok = lambda c: pl.BlockSpec((1, tm, c), lambda bi, i: (bi, i, 0))
    up_spec, un_spec = _halo_specs(tm, l, SC_W)
    weights = [lw["wrsg"], lw["glag"], lw["wbra"], lw["wbrb"], lw["wbrc"], lw["scw"], lw["wo"]]
    return pl.pallas_call(
        _merge_kernel,
        grid=(b, l // tm),
        in_specs=[tok(d), pl.BlockSpec((1, 6, d), lambda bi, i: (bi, 0, 0)), _const_spec(lw["g1"].shape),
                  tok(V_W), tok(V_W), tok(MLA_HEADS * MLA_DV), tok(SC_W), up_spec, un_spec]
                 + [_const_spec(w.shape) for w in weights],
        out_specs=tok(d),
        out_shape=jax.ShapeDtypeStruct((b, l, d), F32),
        compiler_params=_params(("parallel", "parallel")),
        name="merge",
    )(x, mod, lw["g1"], o_f, o_b, o_att, u, u, u, *weights)


def _ffn_kernel(x_ref, xp_ref, xn_ref, mod_ref, g2_ref, wg_ref, wu_ref, cw_ref, wd_ref, gf_ref,
                xo_ref, gs_ref, a_ref, *, fc, final):
    i = pl.program_id(1)
    x = x_ref[0]
    m = mod_ref[0]
    tm = x.shape[0]
    nf = wg_ref.shape[1]

    def hmod(t):
        return _rms(t, g2_ref[...]) * (1.0 + m[4:5]) + m[3:4]

    h = hmod(x)
    hb = h.astype(BF)
    h_prev = jnp.where(i > 0, hmod(xp_ref[0]), 0.0)
    h_next = jnp.where(i < pl.num_programs(1) - 1, hmod(xn_ref[0]), 0.0)
    h_ext = jnp.concatenate([h_prev, h, h_next], axis=0).astype(BF)
    nchunks = nf // fc

    def gate_up(ci):
        cols = slice(ci * fc, (ci + 1) * fc)
        return _dot(h_ext, wg_ref[:, cols]), _dot(hb, wu_ref[:, cols])

    ahead = gate_up(0)
    for ci in range(nchunks):
        cols = slice(ci * fc, (ci + 1) * fc)
        g_ext, up = ahead
        if ci + 1 < nchunks:
            ahead = gate_up(ci + 1)
        cw = cw_ref[:, cols]
        gs = gs_ref.at[ci % 2]
        gs[...] = g_ext
        g = (gs[HALO - 1:HALO - 1 + tm, :] * cw[0:1] + gs[HALO:HALO + tm, :] * cw[1:2]
             + gs[HALO + 1:HALO + 1 + tm, :] * cw[2:3] + cw[3:4])
        a_ref[:, cols] = (_silu(g) * up).astype(BF)
    acc = _dot(a_ref[...], wd_ref[...])
    y = x + m[5:6] * acc
    if final:
        y = _rms(y, gf_ref[...])
    xo_ref[0] = y


def _ffn_call(x, mod, lw, gfinal, tm, final):
    b, l, d = x.shape
    nf = lw["wg"].shape[1]
    fc = 256
    tok = pl.BlockSpec((1, tm, d), lambda bi, i: (bi, i, 0))
    xp_spec, xn_spec = _halo_specs(tm, l, d)
    weights = [lw["g2"], lw["wg"], lw["wu"], lw["cw"], lw["wd"], gfinal]
    return pl.pallas_call(
        functools.partial(_ffn_kernel, fc=fc, final=final),
        grid=(b, l // tm),
        in_specs=[tok, xp_spec, xn_spec, pl.BlockSpec((1, 6, d), lambda bi, i: (bi, 0, 0))]
                 + [_const_spec(w.shape) for w in weights],
        out_specs=tok,
        out_shape=jax.ShapeDtypeStruct((b, l, d), F32),
        scratch_shapes=[pltpu.VMEM((2, tm + 2 * HALO, fc), F32), pltpu.VMEM((tm, nf), BF)],
        compiler_params=_params(("parallel", "parallel")),
        name="ffn",
    )(x, x, x, mod, *weights)


def _rope_perm():
    idx = np.zeros((MLA_ROPE,), np.int32)
    sgn = np.zeros((MLA_ROPE,), np.float32)
    for ax in range(2):
        for f in range(ROPE_FREQS):
            lo = ax * 2 * ROPE_FREQS + f
            hi = lo + ROPE_FREQS
            idx[lo], sgn[lo] = hi, -1.0
            idx[hi], sgn[hi] = lo, 1.0
    return idx, sgn


def _rope_tables(l, rotate):
    if rotate:
        rows = l // GRID_W
        row = jnp.repeat(jnp.arange(rows, dtype=F32), GRID_W)
        col = jnp.tile(jnp.arange(GRID_W, dtype=F32), rows)
        inv = ROPE_THETA ** (-jnp.arange(ROPE_FREQS, dtype=F32) / ROPE_FREQS)
        ang = jnp.stack([row[:, None] * inv, col[:, None] * inv], axis=1)
        cos, sin = jnp.cos(ang), jnp.sin(ang)
        expand = lambda t: jnp.broadcast_to(t[:, :, None, :], (l, 2, 2, ROPE_FREQS)).reshape(l, MLA_ROPE)
        cos32, sin32 = expand(cos), expand(sin)
    else:
        cos32, sin32 = jnp.ones((l, MLA_ROPE), F32), jnp.zeros((l, MLA_ROPE), F32)
    cosq = jnp.tile(cos32, (1, MLA_HEADS)).T
    sinq = jnp.tile(sin32, (1, MLA_HEADS)).T
    pad = lambda t: jnp.pad(t, ((0, 0), (0, 128 - MLA_ROPE)))
    return cosq, sinq, pad(cos32), pad(sin32)


def _layer_weights(l, p):
    d = p["w_in"].shape[1]
    w_in = p["w_in"][l]
    sizes = (QK_W, QK_W, V_W, V_W, 2 * GLA_RANK, MLA_Q_LORA, MLA_KV_LORA, MLA_ROPE, SC_W, SC_W, SC_W,
             N_BRANCH * d)
    offs = np.concatenate([[0], np.cumsum(sizes)])
    col = lambda a, b: w_in[:, int(offs[a]):int(offs[b])]
    idx, sgn = _rope_perm()
    w_kr = col(7, 8)
    w_kr_rot = w_kr[:, idx] * sgn
    zeros = lambda n: jnp.zeros((d, n), F32)
    wgk = jnp.zeros((128, 2 * QK_W), F32)
    wgk = wgk.at[MLA_ROPE:MLA_ROPE + GLA_RANK, :QK_W].set(p["w_gk2"][l, 0])
    wgk = wgk.at[MLA_ROPE + GLA_RANK:MLA_ROPE + 2 * GLA_RANK, QK_W:].set(p["w_gk2"][l, 1])
    w_uq = p["w_uq"][l].reshape(MLA_Q_LORA, MLA_HEADS, MLA_NOPE + MLA_ROPE)
    w_uq_rope = w_uq[:, :, MLA_NOPE:]
    w_ukv = p["w_ukv"][l].reshape(MLA_KV_LORA, MLA_HEADS, MLA_NOPE + MLA_DV)
    nf = p["w_ffn_gate"].shape[2]
    cw = jnp.concatenate([p["ffn_conv_w"][l], p["ffn_conv_b"][l][None], jnp.zeros((4, nf), F32)], axis=0)
    scw = jnp.concatenate([p["sc_w"][l], jnp.zeros((5, SC_W), F32)], axis=0)
    bf = lambda t: t.astype(BF)
    return {
        "g1": p["norm1_g"][l][None],
        "wqkv": bf(col(0, 3)),
        "wsm1": bf(jnp.concatenate([w_kr, col(4, 5), zeros(128 - MLA_ROPE - 2 * GLA_RANK)], axis=1)),
        "wsm2": bf(jnp.concatenate([w_kr_rot, zeros(128 - MLA_ROPE)], axis=1)),
        "wgk": bf(wgk),
        "bgk": p["b_gk2"][l].reshape(1, 2 * QK_W),
        "wcq": bf(col(5, 6)),
        "gq": p["mla_q_norm_g"][l][None],
        "wuqn": bf(w_uq[:, :, :MLA_NOPE].reshape(MLA_Q_LORA, -1).T),
        "wuk": bf(jnp.transpose(w_ukv[:, :, :MLA_NOPE], (1, 0, 2))),
        "wuqr": bf(w_uq_rope.reshape(MLA_Q_LORA, -1).T),
        "wuqrr": bf((w_uq_rope[:, :, idx] * sgn).reshape(MLA_Q_LORA, -1).T),
        "wckv": bf(col(6, 7)),
        "gkv": p["mla_kv_norm_g"][l][None],
        "ws": bf(col(9, 11)),
        "wuvt": bf(jnp.transpose(w_ukv[:, :, MLA_NOPE:], (1, 2, 0)).reshape(MLA_HEADS * MLA_DV, MLA_KV_LORA)),
        "wrsg": bf(jnp.concatenate([col(3, 4), col(8, 9), col(11, 12)], axis=1)),
        "glag": p["gla_norm_g"][l][None],
        "wbra": bf(p["w_br_a"][l]),
        "wbrb": bf(p["w_br_b"][l]),
        "wbrc": bf(p["w_br_c"][l]),
        "scw": scw,
        "wo": bf(p["w_o"][l]),
        "g2": p["norm2_g"][l][None],
        "wg": bf(p["w_ffn_gate"][l]),
        "wu": bf(p["w_ffn_up"][l]),
        "cw": cw,
        "wd": bf(p["w_ffn_down"][l]),
    }


def _pick(n, options):
    for o in options:
        if n % o == 0:
            return o
    raise ValueError(f"no tile in {options} divides {n}")


def kernel(x, c, ctx, c_ctx, w_ada, b_ada, norm1_g, w_in, w_gk2, b_gk2, gla_norm_g, mla_q_norm_g, w_uq,
           mla_kv_norm_g, w_ukv, sc_w, w_br_a, w_br_b, w_br_c, w_o, norm2_g, w_ffn_gate, w_ffn_up,
           ffn_conv_w, ffn_conv_b, w_ffn_down, final_norm_g):
    p = dict(w_in=w_in, w_gk2=w_gk2, b_gk2=b_gk2, gla_norm_g=gla_norm_g, mla_q_norm_g=mla_q_norm_g,
             w_uq=w_uq, mla_kv_norm_g=mla_kv_norm_g, w_ukv=w_ukv, sc_w=sc_w, w_br_a=w_br_a,
             w_br_b=w_br_b, w_br_c=w_br_c, w_o=w_o, norm1_g=norm1_g, norm2_g=norm2_g,
             w_ffn_gate=w_ffn_gate, w_ffn_up=w_ffn_up, ffn_conv_w=ffn_conv_w, ffn_conv_b=ffn_conv_b,
             w_ffn_down=w_ffn_down)
    b, l, d = x.shape
    lc = ctx.shape[1]
    depth = w_in.shape[0]
    assert l % PAIR == 0 and lc % PAIR == 0 and l % GRID_W == 0

    tiles = (512, 256, 128)
    tm, tm_c = _pick(l, tiles), _pick(lc, tiles)
    tm_p = _pick(l, (1024,) + tiles)
    tb, tb_c = _pick(l, (1024,) + tiles), _pick(lc, tiles)
    tq, tq_c = _pick(l, tiles), _pick(lc, tiles)
    key_tiles = (3 * KEY_SUB, 2 * KEY_SUB, KEY_SUB)
    tk, tk_c = _pick(l + lc, key_tiles), _pick(lc, key_tiles)

    rows = -(-(b + 1) // 8) * 8
    cc = jnp.concatenate([c, c_ctx[None], jnp.zeros((rows - b - 1, d), F32)], axis=0)
    mod_all = _mod_call(cc, w_ada, b_ada)
    tabs = _rope_tables(l, True)
    tabs_c = _rope_tables(lc, False)
    gfinal = final_norm_g[None]
    zero_state = jnp.zeros((b, 2, V_W, QK_W), F32)

    xc = ctx
    for li in range(depth):
        last = li == depth - 1
        lw = _layer_weights(li, p)
        mod = mod_all[li, :b].reshape(b, 6, d)
        mod_c = jnp.broadcast_to(mod_all[li, b].reshape(1, 6, d), (b, 6, d))

        qk_c, v_c, vt_c, la_c, qt_c, kabs_c, vm_c, u_c = _proj_call(xc, mod_c, lw, tabs_c, tm_c)
        of_c, ob_c, s_ctx = _gla_call(qk_c, v_c, vt_c, la_c, zero_state, tb_c)

        qk, v, vt, la, qt, kabs, vm, u = _proj_call(x, mod, lw, tabs, tm_p)
        o_f, o_b, _ = _gla_call(qk, v, vt, la, s_ctx, tb)
        k_all = jnp.concatenate([kabs, kabs_c], axis=1)
        vm_all = jnp.concatenate([vm, vm_c], axis=2)
        o_att = _attn_call(qt, k_all, vm_all, tq, tk)
        x = _merge_call(x, mod, o_f, o_b, o_att, u, lw, tm)
        if not last:
            o_att_c = _attn_call(qt_c, kabs_c, vm_c, tq_c, tk_c)
            xc = _merge_call(xc, mod_c, of_c, ob_c, o_att_c, u_c, lw, tm_c)
        x = _ffn_call(x, mod, lw, gfinal, tm, last)
        if not last:
            xc = _ffn_call(xc, mod_c, lw, gfinal, tm_c, False)
    return x
```

```python
import functools

import numpy as np
import jax
import jax.numpy as jnp
from jax import lax
from jax.experimental import pallas as pl
from jax.experimental.pallas import tpu as pltpu

F32 = jnp.float32
BF = jnp.bfloat16

GRID_W = 64
GLA_HEADS = 4
GLA_DK = 64
GLA_DV = 128
GLA_RANK = 16
GLA_GATE_NORM = 16.0
GLA_CHUNK = 64
MLA_HEADS = 8
MLA_Q_LORA = 256
MLA_KV_LORA = 128
MLA_NOPE = 64
MLA_ROPE = 32
MLA_DV = 64
ROPE_THETA = 10000.0
ROPE_FREQS = MLA_ROPE // 4
SC_W = 512
NORM_EPS = 1e-6
N_BRANCH = 3

QK_W = GLA_HEADS * GLA_DK
V_W = GLA_HEADS * GLA_DV
Q_ABS = MLA_KV_LORA + MLA_ROPE
PAIR = 2 * GLA_CHUNK
HALO = 8
NEG_BIG = -1e30
ONES_ROWS = 16
V_ROWS = MLA_DV + ONES_ROWS
LOG2E = 1.4426950408889634
KEY_SUB = 256
ATTN_SKEW = 2
SCORE_SLOTS = 4
assert MLA_HEADS % SCORE_SLOTS == 0 and SCORE_SLOTS > ATTN_SKEW

VMEM_LIMIT = 56 * 1024 * 1024


def _dot(a, b):
    return jnp.dot(a, b, preferred_element_type=F32)


def _dot_nt(a, b):
    return lax.dot_general(a, b, (((1,), (1,)), ((), ())), preferred_element_type=F32)


def _rms(x, g):
    return x * lax.rsqrt(jnp.mean(x * x, axis=-1, keepdims=True) + NORM_EPS) * g


def _silu(x):
    return x * jax.nn.sigmoid(x)


def _const_spec(shape):
    nd = len(shape)
    return pl.BlockSpec(shape, lambda *_: (0,) * nd, pipeline_mode=pl.Buffered(1))


def _params(sem):
    return pltpu.CompilerParams(dimension_semantics=sem, vmem_limit_bytes=VMEM_LIMIT)


def _mod_kernel(c_ref, w_ref, b_ref, o_ref):
    a = _silu(c_ref[...]).astype(BF)
    o_ref[0] = _dot(a, w_ref[0].astype(BF)) + b_ref[0]


def _mod_call(cc, w_ada, b_ada):
    nl, d, n = w_ada.shape
    rows = cc.shape[0]
    tn = 1024
    return pl.pallas_call(
        _mod_kernel,
        grid=(nl, n // tn),
        in_specs=[
            pl.BlockSpec((rows, d), lambda l, j: (0, 0)),
            pl.BlockSpec((1, d, tn), lambda l, j: (l, 0, j)),
            pl.BlockSpec((1, 1, tn), lambda l, j: (l, 0, j)),
        ],
        out_specs=pl.BlockSpec((1, rows, tn), lambda l, j: (l, 0, j)),
        out_shape=jax.ShapeDtypeStruct((nl, rows, n), F32),
        compiler_params=_params(("parallel", "parallel")),
        name="adaln_mod",
    )(cc, w_ada, b_ada.reshape(nl, 1, n))


def _proj_kernel(x_ref, mod_ref, g1_ref, wqkv_ref, wsm1_ref, wsm2_ref, wgk_ref, bgk_ref,
                 wcq_ref, gq_ref, wuqn_ref, wuk_ref, wuqr_ref, wuqrr_ref, wckv_ref, gkv_ref, wuvt_ref,
                 ws_ref, cosq_ref, sinq_ref, cosk_ref, sink_ref,
                 qk_ref, v_ref, vt_ref, la_ref, qt_ref, kabs_ref, vt_mla_ref, u_ref):
    x = x_ref[0]
    m = mod_ref[0]
    h = _rms(x, g1_ref[...]) * (1.0 + m[1:2]) + m[0:1]
    hb = h.astype(BF)

    sm1 = _dot(hb, wsm1_ref[...])
    sm2 = _dot(hb, wsm2_ref[...])
    cq = _dot(hb, wcq_ref[...])
    ckv = _dot(hb, wckv_ref[...])

    qkv = _dot(hb, wqkv_ref[...])

    z = _dot(sm1.astype(BF), wgk_ref[...]) + bgk_ref[...]
    la_ref[0] = jax.nn.log_sigmoid(z) * (1.0 / GLA_GATE_NORM)

    scale = (MLA_NOPE + MLA_ROPE) ** -0.5 * LOG2E
    cqn = _rms(cq, gq_ref[...]).astype(BF)
    qnt = _dot_nt(wuqn_ref[...], cqn).astype(BF)
    qr_t = _dot_nt(wuqr_ref[...], cqn)
    qrr_t = _dot_nt(wuqrr_ref[...], cqn)

    s = _dot(hb, ws_ref[...])

    qk_ref[0] = qkv[:, :2 * QK_W].astype(BF)
    v = qkv[:, 2 * QK_W:]
    v_ref[0] = v.astype(BF)
    vt_ref[0] = v.T.astype(BF)

    qrope_t = (qr_t * cosq_ref[...] + qrr_t * sinq_ref[...]) * scale
    for hh in range(MLA_HEADS):
        qa = _dot(wuk_ref[hh], qnt[MLA_NOPE * hh:MLA_NOPE * (hh + 1), :]) * scale
        qt_ref[0, Q_ABS * hh:Q_ABS * hh + MLA_KV_LORA, :] = qa.astype(BF)
        qt_ref[0, Q_ABS * hh + MLA_KV_LORA:Q_ABS * (hh + 1), :] = (
            qrope_t[MLA_ROPE * hh:MLA_ROPE * (hh + 1), :].astype(BF))

    cn = _rms(ckv, gkv_ref[...]).astype(BF)
    kabs_ref[0, :, 0:MLA_KV_LORA] = cn
    kr = sm1 * cosk_ref[...] + sm2 * sink_ref[...]
    kabs_ref[0, :, MLA_KV_LORA:Q_ABS] = kr[:, 0:MLA_ROPE].astype(BF)
    vt_mla_ref[0] = _dot_nt(wuvt_ref[...], cn).astype(BF)

    u_ref[0] = s[:, :SC_W] * s[:, SC_W:]


def _proj_call(x, mod, lw, tabs, tm):
    b, l, d = x.shape
    cosq, sinq, cosk, sink = tabs
    tok = lambda c: pl.BlockSpec((1, tm, c), lambda bi, i: (bi, i, 0))
    tr = lambda r: pl.BlockSpec((1, r, tm), lambda bi, i: (bi, 0, i))
    weights = [lw["g1"], lw["wqkv"], lw["wsm1"], lw["wsm2"], lw["wgk"], lw["bgk"], lw["wcq"],
               lw["gq"], lw["wuqn"], lw["wuk"], lw["wuqr"], lw["wuqrr"], lw["wckv"], lw["gkv"],
               lw["wuvt"], lw["ws"]]
    in_specs = ([tok(d), pl.BlockSpec((1, 6, d), lambda bi, i: (bi, 0, 0))]
                + [_const_spec(w.shape) for w in weights]
                + [pl.BlockSpec((MLA_HEADS * MLA_ROPE, tm), lambda bi, i: (0, i)),
                   pl.BlockSpec((MLA_HEADS * MLA_ROPE, tm), lambda bi, i: (0, i)),
                   pl.BlockSpec((tm, 128), lambda bi, i: (i, 0)),
                   pl.BlockSpec((tm, 128), lambda bi, i: (i, 0))])
    out_shape = (
        jax.ShapeDtypeStruct((b, l, 2 * QK_W), BF),
        jax.ShapeDtypeStruct((b, l, V_W), BF),
        jax.ShapeDtypeStruct((b, V_W, l), BF),
        jax.ShapeDtypeStruct((b, l, 2 * QK_W), F32),
        jax.ShapeDtypeStruct((b, MLA_HEADS * Q_ABS, l), BF),
        jax.ShapeDtypeStruct((b, l, Q_ABS), BF),
        jax.ShapeDtypeStruct((b, MLA_HEADS * MLA_DV, l), BF),
        jax.ShapeDtypeStruct((b, l, SC_W), F32),
    )
    out_specs = (tok(2 * QK_W), tok(V_W), tr(V_W), tok(2 * QK_W), tr(MLA_HEADS * Q_ABS),
                 tok(Q_ABS), tr(MLA_HEADS * MLA_DV), tok(SC_W))
    return pl.pallas_call(
        _proj_kernel,
        grid=(b, l // tm),
        in_specs=in_specs,
        out_specs=out_specs,
        out_shape=out_shape,
        compiler_params=_params(("parallel", "parallel")),
        name="proj",
    )(x, mod, *weights, cosq, sinq, cosk, sink)


def _gla_kernel(qkf_ref, qkb_ref, vf_ref, vb_ref, vtf_ref, vtb_ref, laf_ref, lab_ref, s0_ref,
                of_ref, ob_ref, sfin_ref, s_ref, *, tb):
    n = pl.program_id(1)
    npairs = tb // PAIR

    @pl.when(n == 0)
    def _():
        s_ref[...] = s0_ref[0]

    r = lax.broadcasted_iota(jnp.int32, (PAIR, PAIR), 0)
    c = lax.broadcasted_iota(jnp.int32, (PAIR, PAIR), 1)
    same = (r // GLA_CHUNK) == (c // GLA_CHUNK)
    ones_c = jnp.where(same, 1.0, 0.0).astype(BF)
    lane_head = lax.broadcasted_iota(jnp.int32, (PAIR, QK_W), 1) // GLA_DK
    row_chunk = lax.broadcasted_iota(jnp.int32, (PAIR, QK_W), 0) // GLA_CHUNK
    bd_mask = (lax.broadcasted_iota(jnp.int32, (V_W, QK_W), 0) // GLA_DV
               == lax.broadcasted_iota(jnp.int32, (V_W, QK_W), 1) // GLA_DK)

    def stage_decay(d):
        tri_b = jnp.where(d["tri"], 1.0, 0.0).astype(BF)
        la = d["la_ref"][0, d["rows"], :]
        hi = la.astype(BF)
        lo = (la - hi.astype(F32)).astype(BF)
        d["bcum"] = _dot(tri_b, hi) + _dot(tri_b, lo)
        d["blast"] = _dot(ones_c, hi) + _dot(ones_c, lo)

    def stage_scale(d):
        qk = d["qk_ref"][0, d["rows"], :].astype(F32)
        q = qk[:, :QK_W] * (GLA_DK ** -0.5)
        k = qk[:, QK_W:]
        late = row_chunk == (1 if d["forward"] else 0)
        blast = d["blast"]
        other = jnp.concatenate([blast[GLA_CHUNK:], blast[:GLA_CHUNK]], axis=0)
        d_other = jnp.exp(other)
        q_in = q * jnp.exp(d["bcum"])
        k_st = k * jnp.exp(blast - d["bcum"])
        d["q_in"] = q_in.astype(BF)
        d["k_in"] = (k * jnp.exp(-d["bcum"])).astype(BF)
        d["k_st"] = k_st.astype(BF)
        d["q_state"] = jnp.where(late, q_in * d_other, q_in).astype(BF)
        d["k_state"] = jnp.where(late, k_st, k_st * d_other).astype(BF)
        d["decay"] = jnp.exp(blast[0:1] + other[0:1])

    def stage_state_inc(d):
        vt = d["vt_ref"][0, :, d["rows"]]
        d["ut"] = jnp.where(bd_mask, _dot(vt, d["k_state"]), 0.0)

    def stage_scores(d):
        q_in = d["q_in"]
        qs = jnp.concatenate(
            [jnp.where(lane_head == hh, q_in, jnp.zeros_like(q_in)) for hh in range(GLA_HEADS)], axis=0)
        a2 = _dot_nt(qs, jnp.concatenate([d["k_in"], d["k_st"]], axis=0))
        cross = ((r >= GLA_CHUNK) & (c < GLA_CHUNK)) if d["forward"] else ((r < GLA_CHUNK) & (c >= GLA_CHUNK))
        tri4 = jnp.concatenate([d["tri"]] * GLA_HEADS, axis=0)
        cross4 = jnp.concatenate([cross] * GLA_HEADS, axis=0)
        d["a"] = jnp.where(tri4, a2[:, :PAIR], jnp.where(cross4, a2[:, PAIR:], 0.0)).astype(BF)

    def stage_intra(d):
        v = d["v_ref"][0, d["rows"], :]
        a = d["a"]
        d["o_intra"] = jnp.concatenate(
            [_dot(a[PAIR * hh:PAIR * (hh + 1)], v[:, GLA_DV * hh:GLA_DV * (hh + 1)])
             for hh in range(GLA_HEADS)], axis=1)

    def stage_scan(d):
        st = s_ref[d["sidx"]]
        d["o_ref"][0, d["rows"], :] = d["o_intra"] + _dot_nt(d["q_state"], st.astype(BF))
        s_ref[d["sidx"]] = st * d["decay"] + d["ut"]

    def body(p, carry):
        dirs = [dict(qk_ref=qkf_ref, v_ref=vf_ref, vt_ref=vtf_ref, la_ref=laf_ref, o_ref=of_ref, sidx=0,
                     forward=True, start=pl.multiple_of(p * PAIR, PAIR)),
                dict(qk_ref=qkb_ref, v_ref=vb_ref, vt_ref=vtb_ref, la_ref=lab_ref, o_ref=ob_ref, sidx=1,
                     forward=False, start=pl.multiple_of((npairs - 1 - p) * PAIR, PAIR))]
        for d in dirs:
            d["rows"] = pl.ds(d["start"], PAIR)
            d["tri"] = jnp.logical_and(same, (c <= r) if d["forward"] else (c >= r))
        for stage in (stage_decay, stage_scale, stage_state_inc, stage_scores, stage_intra, stage_scan):
            for d in dirs:
                stage(d)
        return carry

    lax.fori_loop(0, npairs, body, 0)

    @pl.when(n == pl.num_programs(1) - 1)
    def _():
        sfin_ref[0] = s_ref[...]


def _gla_call(qk, v, vt, la, s0, tb):
    b, l, _ = qk.shape
    nb = l // tb
    fwd = lambda c: pl.BlockSpec((1, tb, c), lambda bi, i: (bi, i, 0))
    bwd = lambda c: pl.BlockSpec((1, tb, c), lambda bi, i: (bi, nb - 1 - i, 0))
    st_spec = pl.BlockSpec((1, 2, V_W, QK_W), lambda bi, i: (bi, 0, 0, 0))
    return pl.pallas_call(
        functools.partial(_gla_kernel, tb=tb),
        grid=(b, nb),
        in_specs=[fwd(2 * QK_W), bwd(2 * QK_W), fwd(V_W), bwd(V_W),
                  pl.BlockSpec((1, V_W, tb), lambda bi, i: (bi, 0, i)),
                  pl.BlockSpec((1, V_W, tb), lambda bi, i: (bi, 0, nb - 1 - i)),
                  pl.BlockSpec((1, tb, QK_W), lambda bi, i: (bi, i, 0)),
                  pl.BlockSpec((1, tb, QK_W), lambda bi, i: (bi, nb - 1 - i, 1)),
                  st_spec],
        out_specs=(fwd(V_W), bwd(V_W), st_spec),
        out_shape=(jax.ShapeDtypeStruct((b, l, V_W), F32),
                   jax.ShapeDtypeStruct((b, l, V_W), F32),
                   jax.ShapeDtypeStruct((b, 2, V_W, QK_W), F32)),
        scratch_shapes=[pltpu.VMEM((2, V_W, QK_W), F32)],
        compiler_params=_params(("parallel", "arbitrary")),
        name="gla",
    )(qk, qk, v, v, vt, vt, la, la, s0)


def _attn_kernel(qt_ref, k_ref, vt_ref, o_ref, m_ref, acc_ref, ot_ref, s_ref, p_ref, *, tk):
    tq = qt_ref.shape[2]
    nchunks = k_ref.shape[1] // tk
    nsub = tk // KEY_SUB
    m_ref[...] = jnp.full(m_ref.shape, NEG_BIG, F32)
    acc_ref[...] = jnp.zeros(acc_ref.shape, F32)

    def score_sub(kstart, hh, i, slot, mpart):
        ks = k_ref[0, pl.ds(kstart + KEY_SUB * i, KEY_SUB), :]
        s = _dot(ks, qt_ref[0, Q_ABS * hh:Q_ABS * (hh + 1), :])
        s_ref[slot, KEY_SUB * i:KEY_SUB * (i + 1), :] = s
        return jnp.maximum(mpart, jnp.max(s.reshape(KEY_SUB // 8, 8, tq), axis=0))

    def score_finish(hh, mpart):
        m_old = m_ref[hh:hh + 1, :]
        m_new = jnp.maximum(m_old, jnp.max(mpart, axis=0, keepdims=True))
        m_ref[hh:hh + 1, :] = m_new
        return m_new, jnp.exp2(m_old - m_new)

    def value_sub(i, slot, pslot, m_new):
        rows = slice(KEY_SUB * i, KEY_SUB * (i + 1))
        p_ref[pslot, rows, :] = jnp.exp2(s_ref[slot, rows, :] - m_new).astype(BF)

    mpart0 = jnp.full((8, tq), NEG_BIG, F32)
    ones_rows = jnp.ones((ONES_ROWS, tk), BF)
    ahead = []
    for hh in range(ATTN_SKEW):
        mp = mpart0
        for i in range(nsub):
            mp = score_sub(0, hh, i, hh % SCORE_SLOTS, mp)
        ahead.extend(score_finish(hh, mp))

    def chunk_body(j, carry):
        pend = [(carry[2 * u], carry[2 * u + 1]) for u in range(ATTN_SKEW)]
        start = pl.multiple_of(j * tk, KEY_SUB)
        start_next = pl.multiple_of(jnp.minimum(j + 1, nchunks - 1) * tk, KEY_SUB)
        for hh in range(MLA_HEADS):
            m_cur, alpha_cur = pend.pop(0)
            nh = (hh + ATTN_SKEW) % MLA_HEADS
            nstart = start if hh + ATTN_SKEW < MLA_HEADS else start_next
            mp = mpart0
            for i in range(nsub):
                value_sub(i, hh % SCORE_SLOTS, hh % 2, m_cur)
                mp = score_sub(nstart, nh, i, (hh + ATTN_SKEW) % SCORE_SLOTS, mp)
            vals = jnp.concatenate(
                [vt_ref[0, MLA_DV * hh:MLA_DV * (hh + 1), pl.ds(start, tk)], ones_rows], axis=0)
            acc_ref[hh] = alpha_cur * acc_ref[hh] + _dot(vals, p_ref[hh % 2])
            pend.append(score_finish(nh, mp))
        return tuple(v for pair in pend for v in pair)

    lax.fori_loop(0, nchunks, chunk_body, tuple(ahead))
    for hh in range(MLA_HEADS):
        a = acc_ref[hh]
        ot_ref[MLA_DV * hh:MLA_DV * (hh + 1), :] = a[:MLA_DV] / a[MLA_DV:MLA_DV + 1]
    o_ref[0] = ot_ref[...].T.astype(BF)


def _attn_call(qt, kabs, vt, tq, tk):
    b, _, l = qt.shape
    lk = kabs.shape[1]
    return pl.pallas_call(
        functools.partial(_attn_kernel, tk=tk),
        grid=(b, l // tq),
        in_specs=[pl.BlockSpec((1, MLA_HEADS * Q_ABS, tq), lambda bi, i: (bi, 0, i)),
                  pl.BlockSpec((1, lk, Q_ABS), lambda bi, i: (bi, 0, 0)),
                  pl.BlockSpec((1, MLA_HEADS * MLA_DV, lk), lambda bi, i: (bi, 0, 0))],
        out_specs=pl.BlockSpec((1, tq, MLA_HEADS * MLA_DV), lambda bi, i: (bi, i, 0)),
        out_shape=jax.ShapeDtypeStruct((b, l, MLA_HEADS * MLA_DV), BF),
        scratch_shapes=[pltpu.VMEM((MLA_HEADS, tq), F32),
                        pltpu.VMEM((MLA_HEADS, V_ROWS, tq), F32),
                        pltpu.VMEM((MLA_HEADS * MLA_DV, tq), F32),
                        pltpu.VMEM((SCORE_SLOTS, tk, tq), F32),
                        pltpu.VMEM((2, tk, tq), BF)],
        compiler_params=_params(("parallel", "parallel")),
        name="attn",
    )(qt, kabs, vt)


def _shift_rows(t, prev_row, next_row):
    n = t.shape[0]
    row = lax.broadcasted_iota(jnp.int32, t.shape, 0)
    down = jnp.where(row == 0, prev_row, pltpu.roll(t, 1, 0))
    up = jnp.where(row == n - 1, next_row, pltpu.roll(t, n - 1, 0))
    return down, up


def _merge_kernel(x_ref, mod_ref, g1_ref, of_ref, ob_ref, oat_ref, u_ref, up_ref, un_ref,
                  wrsg_ref, glag_ref, wbra_ref, wbrb_ref, wbrc_ref, scw_ref, wo_ref, xo_ref):
    i = pl.program_id(1)
    x = x_ref[0]
    m = mod_ref[0]
    d = x.shape[1]
    hb = (_rms(x, g1_ref[...]) * (1.0 + m[1:2]) + m[0:1]).astype(BF)
    rsg = _dot(hb, wrsg_ref[...])
    r_a = rsg[:, :V_W]
    sb = rsg[:, V_W:V_W + SC_W]
    gates = jax.nn.sigmoid(rsg[:, V_W + SC_W:])

    o = of_ref[0] + ob_ref[0]
    gg = glag_ref[...]
    on = jnp.concatenate(
        [_rms(o[:, GLA_DV * hh:GLA_DV * (hh + 1)], gg[:, GLA_DV * hh:GLA_DV * (hh + 1)])
         for hh in range(GLA_HEADS)], axis=1)
    y_a = _dot((on * _silu(r_a)).astype(BF), wbra_ref[...])
    y_b = _dot(oat_ref[0], wbrb_ref[...])

    u = u_ref[0]
    prev_row = jnp.where(i > 0, up_ref[0, HALO - 1:HALO, :], 0.0)
    next_row = jnp.where(i < pl.num_programs(1) - 1, un_ref[0, 0:1, :], 0.0)
    u_dn, u_up = _shift_rows(u, prev_row, next_row)
    w = scw_ref[...]
    conv = u_dn * w[0:1] + u * w[1:2] + u_up * w[2:3]
    y_c = _dot((sb * conv).astype(BF), wbrc_ref[...])

    mix = gates[:, :d] * y_a + gates[:, d:2 * d] * y_b + gates[:, 2 * d:] * y_c
    xo_ref[0] = x + m[2:3] * _dot(mix.astype(BF), wo_ref[...])


def _halo_specs(tm, l, c):
    nb8 = tm // HALO
    last8 = l // HALO - 1
    prev = pl.BlockSpec((1, HALO, c), lambda bi, i: (bi, jnp.maximum(i * nb8 - 1, 0), 0))
    nxt = pl.BlockSpec((1, HALO, c), lambda bi, i: (bi, jnp.minimum((i + 1) * nb8, last8), 0))
    return prev, nxt


def _merge_call(x, mod, o_f, o_b, o_att, u, lw, tm):
    b, l, d = x.shape
    tok = lambda c: pl.BlockSpec((1, tm, c), lambda bi, i: (bi, i, 0))
    up_spec, un_spec = _halo_specs(tm, l, SC_W)
    weights = [lw["wrsg"], lw["glag"], lw["wbra"], lw["wbrb"], lw["wbrc"], lw["scw"], lw["wo"]]
    return pl.pallas_call(
        _merge_kernel,
        grid=(b, l // tm),
        in_specs=[tok(d), pl.BlockSpec((1, 6, d), lambda bi, i: (bi, 0, 0)), _const_spec(lw["g1"].shape),
                  tok(V_W), tok(V_W), tok(MLA_HEADS * MLA_DV), tok(SC_W), up_spec, un_spec]
                 + [_const_spec(w.shape) for w in weights],
        out_specs=tok(d),
        out_shape=jax.ShapeDtypeStruct((b, l, d), F32),
        compiler_params=_params(("parallel", "parallel")),
        name="merge",
    )(x, mod, lw["g1"], o_f, o_b, o_att, u, u, u, *weights)


def _ffn_kernel(x_ref, xp_ref, xn_ref, mod_ref, g2_ref, wg_ref, wu_ref, cw_ref, wd_ref, gf_ref,
                xo_ref, gs_ref, a_ref, *, fc, final):
    i = pl.program_id(1)
    x = x_ref[0]
    m = mod_ref[0]
    tm = x.shape[0]
    nf = wg_ref.shape[1]

    def hmod(t):
        return _rms(t, g2_ref[...]) * (1.0 + m[4:5]) + m[3:4]

    h = hmod(x)
    hb = h.astype(BF)
    h_prev = jnp.where(i > 0, hmod(xp_ref[0]), 0.0)
    h_next = jnp.where(i < pl.num_programs(1) - 1, hmod(xn_ref[0]), 0.0)
    h_ext = jnp.concatenate([h_prev, h, h_next], axis=0).astype(BF)
    nchunks = nf // fc

    def gate_up(ci):
        cols = slice(ci * fc, (ci + 1) * fc)
        return _dot(h_ext, wg_ref[:, cols]), _dot(hb, wu_ref[:, cols])

    ahead = gate_up(0)
    for ci in range(nchunks):
        cols = slice(ci * fc, (ci + 1) * fc)
        g_ext, up = ahead
        if ci + 1 < nchunks:
            ahead = gate_up(ci + 1)
        cw = cw_ref[:, cols]
        gs = gs_ref.at[ci % 2]
        gs[...] = g_ext
        g = (gs[HALO - 1:HALO - 1 + tm, :] * cw[0:1] + gs[HALO:HALO + tm, :] * cw[1:2]
             + gs[HALO + 1:HALO + 1 + tm, :] * cw[2:3] + cw[3:4])
        a_ref[:, cols] = (_silu(g) * up).astype(BF)
    acc = _dot(a_ref[...], wd_ref[...])
    y = x + m[5:6] * acc
    if final:
        y = _rms(y, gf_ref[...])
    xo_ref[0] = y


def _ffn_call(x, mod, lw, gfinal, tm, final):
    b, l, d = x.shape
    nf = lw["wg"].shape[1]
    fc = 256
    tok = pl.BlockSpec((1, tm, d), lambda bi, i: (bi, i, 0))
    xp_spec, xn_spec = _halo_specs(tm, l, d)
    weights = [lw["g2"], lw["wg"], lw["wu"], lw["cw"], lw["wd"], gfinal]
    return pl.pallas_call(
        functools.partial(_ffn_kernel, fc=fc, final=final),
        grid=(b, l // tm),
        in_specs=[tok, xp_spec, xn_spec, pl.BlockSpec((1, 6, d), lambda bi, i: (bi, 0, 0))]
                 + [_const_spec(w.shape) for w in weights],
        out_specs=tok,
        out_shape=jax.ShapeDtypeStruct((b, l, d), F32),
        scratch_shapes=[pltpu.VMEM((2, tm + 2 * HALO, fc), F32), pltpu.VMEM((tm, nf), BF)],
        compiler_params=_params(("parallel", "parallel")),
        name="ffn",
    )(x, x, x, mod, *weights)


def _rope_perm():
    idx = np.zeros((MLA_ROPE,), np.int32)
    sgn = np.zeros((MLA_ROPE,), np.float32)
    for ax in range(2):
        for f in range(ROPE_FREQS):
            lo = ax * 2 * ROPE_FREQS + f
            hi = lo + ROPE_FREQS
            idx[lo], sgn[lo] = hi, -1.0
            idx[hi], sgn[hi] = lo, 1.0
    return idx, sgn


def _rope_tables(l, rotate):
    if rotate:
        rows = l // GRID_W
        row = jnp.repeat(jnp.arange(rows, dtype=F32), GRID_W)
        col = jnp.tile(jnp.arange(GRID_W, dtype=F32), rows)
        inv = ROPE_THETA ** (-jnp.arange(ROPE_FREQS, dtype=F32) / ROPE_FREQS)
        ang = jnp.stack([row[:, None] * inv, col[:, None] * inv], axis=1)
        cos, sin = jnp.cos(ang), jnp.sin(ang)
        expand = lambda t: jnp.broadcast_to(t[:, :, None, :], (l, 2, 2, ROPE_FREQS)).reshape(l, MLA_ROPE)
        cos32, sin32 = expand(cos), expand(sin)
    else:
        cos32, sin32 = jnp.ones((l, MLA_ROPE), F32), jnp.zeros((l, MLA_ROPE), F32)
    cosq = jnp.tile(cos32, (1, MLA_HEADS)).T
    sinq = jnp.tile(sin32, (1, MLA_HEADS)).T
    pad = lambda t: jnp.pad(t, ((0, 0), (0, 128 - MLA_ROPE)))
    return cosq, sinq, pad(cos32), pad(sin32)


def _layer_weights(l, p):
    d = p["w_in"].shape[1]
    w_in = p["w_in"][l]
    sizes = (QK_W, QK_W, V_W, V_W, 2 * GLA_RANK, MLA_Q_LORA, MLA_KV_LORA, MLA_ROPE, SC_W, SC_W, SC_W,
             N_BRANCH * d)
    offs = np.concatenate([[0], np.cumsum(sizes)])
    col = lambda a, b: w_in[:, int(offs[a]):int(offs[b])]
    idx, sgn = _rope_perm()
    w_kr = col(7, 8)
    w_kr_rot = w_kr[:, idx] * sgn
    zeros = lambda n: jnp.zeros((d, n), F32)
    wgk = jnp.zeros((128, 2 * QK_W), F32)
    wgk = wgk.at[MLA_ROPE:MLA_ROPE + GLA_RANK, :QK_W].set(p["w_gk2"][l, 0])
    wgk = wgk.at[MLA_ROPE + GLA_RANK:MLA_ROPE + 2 * GLA_RANK, QK_W:].set(p["w_gk2"][l, 1])
    w_uq = p["w_uq"][l].reshape(MLA_Q_LORA, MLA_HEADS, MLA_NOPE + MLA_ROPE)
    w_uq_rope = w_uq[:, :, MLA_NOPE:]
    w_ukv = p["w_ukv"][l].reshape(MLA_KV_LORA, MLA_HEADS, MLA_NOPE + MLA_DV)
    nf = p["w_ffn_gate"].shape[2]
    cw = jnp.concatenate([p["ffn_conv_w"][l], p["ffn_conv_b"][l][None], jnp.zeros((4, nf), F32)], axis=0)
    scw = jnp.concatenate([p["sc_w"][l], jnp.zeros((5, SC_W), F32)], axis=0)
    bf = lambda t: t.astype(BF)
    return {
        "g1": p["norm1_g"][l][None],
        "wqkv": bf(col(0, 3)),
        "wsm1": bf(jnp.concatenate([w_kr, col(4, 5), zeros(128 - MLA_ROPE - 2 * GLA_RANK)], axis=1)),
        "wsm2": bf(jnp.concatenate([w_kr_rot, zeros(128 - MLA_ROPE)], axis=1)),
        "wgk": bf(wgk),
        "bgk": p["b_gk2"][l].reshape(1, 2 * QK_W),
        "wcq": bf(col(5, 6)),
        "gq": p["mla_q_norm_g"][l][None],
        "wuqn": bf(w_uq[:, :, :MLA_NOPE].reshape(MLA_Q_LORA, -1).T),
        "wuk": bf(jnp.transpose(w_ukv[:, :, :MLA_NOPE], (1, 0, 2))),
        "wuqr": bf(w_uq_rope.reshape(MLA_Q_LORA, -1).T),
        "wuqrr": bf((w_uq_rope[:, :, idx] * sgn).reshape(MLA_Q_LORA, -1).T),
        "wckv": bf(col(6, 7)),
        "gkv": p["mla_kv_norm_g"][l][None],
        "ws": bf(col(9, 11)),
        "wuvt": bf(jnp.transpose(w_ukv[:, :, MLA_NOPE:], (1, 2, 0)).reshape(MLA_HEADS * MLA_DV, MLA_KV_LORA)),
        "wrsg": bf(jnp.concatenate([col(3, 4), col(8, 9), col(11, 12)], axis=1)),
        "glag": p["gla_norm_g"][l][None],
        "wbra": bf(p["w_br_a"][l]),
        "wbrb": bf(p["w_br_b"][l]),
        "wbrc": bf(p["w_br_c"][l]),
        "scw": scw,
        "wo": bf(p["w_o"][l]),
        "g2": p["norm2_g"][l][None],
        "wg": bf(p["w_ffn_gate"][l]),
        "wu": bf(p["w_ffn_up"][l]),
        "cw": cw,
        "wd": bf(p["w_ffn_down"][l]),
    }


def _pick(n, options):
    for o in options:
        if n % o == 0:
            return o
    raise ValueError(f"no tile in {options} divides {n}")


def kernel(x, c, ctx, c_ctx, w_ada, b_ada, norm1_g, w_in, w_gk2, b_gk2, gla_norm_g, mla_q_norm_g, w_uq,
           mla_kv_norm_g, w_ukv, sc_w, w_br_a, w_br_b, w_br_c, w_o, norm2_g, w_ffn_gate, w_ffn_up,
           ffn_conv_w, ffn_conv_b, w_ffn_down, final_norm_g):
    p = dict(w_in=w_in, w_gk2=w_gk2, b_gk2=b_gk2, gla_norm_g=gla_norm_g, mla_q_norm_g=mla_q_norm_g,
             w_uq=w_uq, mla_kv_norm_g=mla_kv_norm_g, w_ukv=w_ukv, sc_w=sc_w, w_br_a=w_br_a,
             w_br_b=w_br_b, w_br_c=w_br_c, w_o=w_o, norm1_g=norm1_g, norm2_g=norm2_g,
             w_ffn_gate=w_ffn_gate, w_ffn_up=w_ffn_up, ffn_conv_w=ffn_conv_w, ffn_conv_b=ffn_conv_b,
             w_ffn_down=w_ffn_down)
    b, l, d = x.shape
    lc = ctx.shape[1]
    depth = w_in.shape[0]
    assert l % PAIR == 0 and lc % PAIR == 0 and l % GRID_W == 0

    tiles = (512, 256, 128)
    tm, tm_c = _pick(l, tiles), _pick(lc, tiles)
    tm_p = _pick(l, (1024,) + tiles)
    tb, tb_c = _pick(l, (1024,) + tiles), _pick(lc, tiles)
    tq, tq_c = _pick(l, tiles), _pick(lc, tiles)
    key_tiles = (3 * KEY_SUB, 2 * KEY_SUB, KEY_SUB)
    tk, tk_c = _pick(l + lc, key_tiles), _pick(lc, key_tiles)

    rows = -(-(b + 1) // 8) * 8
    cc = jnp.concatenate([c, c_ctx[None], jnp.zeros((rows - b - 1, d), F32)], axis=0)
    mod_all = _mod_call(cc, w_ada, b_ada)
    tabs = _rope_tables(l, True)
    tabs_c = _rope_tables(lc, False)
    gfinal = final_norm_g[None]
    zero_state = jnp.zeros((b, 2, V_W, QK_W), F32)

    xc = ctx
    for li in range(depth):
        last = li == depth - 1
        lw = _layer_weights(li, p)
        mod = mod_all[li, :b].reshape(b, 6, d)
        mod_c = jnp.broadcast_to(mod_all[li, b].reshape(1, 6, d), (b, 6, d))

        qk_c, v_c, vt_c, la_c, qt_c, kabs_c, vm_c, u_c = _proj_call(xc, mod_c, lw, tabs_c, tm_c)
        of_c, ob_c, s_ctx = _gla_call(qk_c, v_c, vt_c, la_c, zero_state, tb_c)

        qk, v, vt, la, qt, kabs, vm, u = _proj_call(x, mod, lw, tabs, tm_p)
        o_f, o_b, _ = _gla_call(qk, v, vt, la, s_ctx, tb)
        k_all = jnp.concatenate([kabs, kabs_c], axis=1)
        vm_all = jnp.concatenate([vm, vm_c], axis=2)
        o_att = _attn_call(qt, k_all, vm_all, tq, tk)
        x = _merge_call(x, mod, o_f, o_b, o_att, u, lw, tm)
        if not last:
            o_att_c = _attn_call(qt_c, kabs_c, vm_c, tq_c, tk_c)
            xc = _merge_call(xc, mod_c, of_c, ob_c, o_att_c, u_c, lw, tm_c)
        x = _ffn_call(x, mod, lw, gfinal, tm, last)
        if not last:
            xc = _ffn_call(xc, mod_c, lw, gfinal, tm_c, False)
    return x
```

```python
import functools

import numpy as np
import jax
import jax.numpy as jnp
from jax import lax
from jax.experimental import pallas as pl
from jax.experimental.pallas import tpu as pltpu

F32 = jnp.float32
BF = jnp.bfloat16

LANES = 128
SUBLANES = 8
MXU_TILE = 256
VMEM_LIMIT = 56 * 1024 * 1024

GRID_W = 64
GLA_HEADS = 4
GLA_DK = 64
GLA_DV = 128
GLA_RANK = 16
GLA_GATE_NORM = 16.0
GLA_CHUNK = 64
MLA_HEADS = 8
MLA_Q_LORA = 256
MLA_KV_LORA = 128
MLA_NOPE = 64
MLA_ROPE = 32
MLA_DV = 64
ROPE_THETA = 10000.0
ROPE_FREQS = MLA_ROPE // 4
SC_W = 512
NORM_EPS = 1e-6
N_BRANCH = 3

QK_W = GLA_HEADS * GLA_DK
V_W = GLA_HEADS * GLA_DV
Q_ABS = MLA_KV_LORA + MLA_ROPE
PAIR = 2 * GLA_CHUNK
HALO = SUBLANES
NEG_BIG = -1e30
ONES_ROWS = 16
LOG2E = 1.4426950408889634
KEY_TILES = (768, 512, 256)
ATTN_SKEW = 2
SCORE_SLOTS = 4
assert MLA_HEADS % SCORE_SLOTS == 0 and SCORE_SLOTS > ATTN_SKEW


def _dot(a, b):
    return jnp.dot(a, b, preferred_element_type=F32)


def _dot_nt(a, b):
    return lax.dot_general(a, b, (((1,), (1,)), ((), ())), preferred_element_type=F32)


def _rms(x, g):
    return x * lax.rsqrt(jnp.mean(x * x, axis=-1, keepdims=True) + NORM_EPS) * g


def _silu(x):
    return x * jax.nn.sigmoid(x)


def _const_spec(shape):
    nd = len(shape)
    return pl.BlockSpec(shape, lambda *_: (0,) * nd, pipeline_mode=pl.Buffered(1))


def _params(sem):
    return pltpu.CompilerParams(dimension_semantics=sem, vmem_limit_bytes=VMEM_LIMIT)


def _mod_kernel(c_ref, w_ref, b_ref, o_ref):
    a = _silu(c_ref[...]).astype(BF)
    o_ref[0] = _dot(a, w_ref[0].astype(BF)) + b_ref[0]


def _mod_call(cc, w_ada, b_ada):
    nl, d, n = w_ada.shape
    rows = cc.shape[0]
    tn = 4 * MXU_TILE
    return pl.pallas_call(
        _mod_kernel,
        grid=(nl, n // tn),
        in_specs=[
            pl.BlockSpec((rows, d), lambda l, j: (0, 0)),
            pl.BlockSpec((1, d, tn), lambda l, j: (l, 0, j)),
            pl.BlockSpec((1, 1, tn), lambda l, j: (l, 0, j)),
        ],
        out_specs=pl.BlockSpec((1, rows, tn), lambda l, j: (l, 0, j)),
        out_shape=jax.ShapeDtypeStruct((nl, rows, n), F32),
        compiler_params=_params(("parallel", "parallel")),
        name="adaln_mod",
    )(cc, w_ada, b_ada.reshape(nl, 1, n))


def _proj_kernel(x_ref, mod_ref, g1_ref, wqkv_ref, wsm1_ref, wsm2_ref, wgk_ref, bgk_ref,
                 wcq_ref, gq_ref, wuqn_ref, wuk_ref, wuqr_ref, wuqrr_ref, wckv_ref, gkv_ref,
                 ws_ref, cosq_ref, sinq_ref, cosk_ref, sink_ref,
                 qk_ref, v_ref, la_ref, qt_ref, kabs_ref, ct_ref, u_ref):
    x = x_ref[0]
    m = mod_ref[0]
    h = _rms(x, g1_ref[...]) * (1.0 + m[1:2]) + m[0:1]
    hb = h.astype(BF)

    sm1 = _dot(hb, wsm1_ref[...])
    sm2 = _dot(hb, wsm2_ref[...])
    cq = _dot(hb, wcq_ref[...])
    ckv = _dot(hb, wckv_ref[...])

    qkv = _dot(hb, wqkv_ref[...])

    z = _dot(sm1.astype(BF), wgk_ref[...]) + bgk_ref[...]
    la_ref[0] = jax.nn.log_sigmoid(z) * (1.0 / GLA_GATE_NORM)

    scale = (MLA_NOPE + MLA_ROPE) ** -0.5 * LOG2E
    cqn = _rms(cq, gq_ref[...]).astype(BF)
    qnt = _dot_nt(wuqn_ref[...], cqn).astype(BF)
    qr_t = _dot_nt(wuqr_ref[...], cqn)
    qrr_t = _dot_nt(wuqrr_ref[...], cqn)

    s = _dot(hb, ws_ref[...])

    qk_ref[0] = qkv[:, :2 * QK_W].astype(BF)
    v = qkv[:, 2 * QK_W:]
    v_ref[0] = v.astype(BF)

    qrope_t = (qr_t * cosq_ref[...] + qrr_t * sinq_ref[...]) * scale
    for hh in range(MLA_HEADS):
        qa = _dot(wuk_ref[hh], qnt[MLA_NOPE * hh:MLA_NOPE * (hh + 1), :]) * scale
        qt_ref[0, Q_ABS * hh:Q_ABS * hh + MLA_KV_LORA, :] = qa.astype(BF)
        qt_ref[0, Q_ABS * hh + MLA_KV_LORA:Q_ABS * (hh + 1), :] = (
            qrope_t[MLA_ROPE * hh:MLA_ROPE * (hh + 1), :].astype(BF))

    cn = _rms(ckv, gkv_ref[...])
    kabs_ref[0, :, 0:MLA_KV_LORA] = cn.astype(BF)
    kr = sm1 * cosk_ref[...] + sm2 * sink_ref[...]
    kabs_ref[0, :, MLA_KV_LORA:Q_ABS] = kr[:, 0:MLA_ROPE].astype(BF)
    ct_ref[0] = cn.T.astype(BF)

    u_ref[0] = s[:, :SC_W] * s[:, SC_W:]


def _proj_call(x, mod, lw, tabs, tm):
    b, l, d = x.shape
    cosq, sinq, cosk, sink = tabs
    tok = lambda c: pl.BlockSpec((1, tm, c), lambda bi, i: (bi, i, 0))
    tr = lambda r: pl.BlockSpec((1, r, tm), lambda bi, i: (bi, 0, i))
    weights = [lw["g1"], lw["wqkv"], lw["wsm1"], lw["wsm2"], lw["wgk"], lw["bgk"], lw["wcq"],
               lw["gq"], lw["wuqn"], lw["wuk"], lw["wuqr"], lw["wuqrr"], lw["wckv"], lw["gkv"],
               lw["ws"]]
    in_specs = ([tok(d), pl.BlockSpec((1, 6, d), lambda bi, i: (bi, 0, 0))]
                + [_const_spec(w.shape) for w in weights]
                + [pl.BlockSpec((MLA_HEADS * MLA_ROPE, tm), lambda bi, i: (0, i)),
                   pl.BlockSpec((MLA_HEADS * MLA_ROPE, tm), lambda bi, i: (0, i)),
                   pl.BlockSpec((tm, LANES), lambda bi, i: (i, 0)),
                   pl.BlockSpec((tm, LANES), lambda bi, i: (i, 0))])
    out_shape = (
        jax.ShapeDtypeStruct((b, l, 2 * QK_W), BF),
        jax.ShapeDtypeStruct((b, l, V_W), BF),
        jax.ShapeDtypeStruct((b, l, 2 * QK_W), F32),
        jax.ShapeDtypeStruct((b, MLA_HEADS * Q_ABS, l), BF),
        jax.ShapeDtypeStruct((b, l, Q_ABS), BF),
        jax.ShapeDtypeStruct((b, MLA_KV_LORA, l), BF),
        jax.ShapeDtypeStruct((b, l, SC_W), F32),
    )
    out_specs = (tok(2 * QK_W), tok(V_W), tok(2 * QK_W), tr(MLA_HEADS * Q_ABS),
                 tok(Q_ABS), tr(MLA_KV_LORA), tok(SC_W))
    return pl.pallas_call(
        _proj_kernel,
        grid=(b, l // tm),
        in_specs=in_specs,
        out_specs=out_specs,
        out_shape=out_shape,
        compiler_params=_params(("parallel", "parallel")),
        name="proj",
    )(x, mod, *weights, cosq, sinq, cosk, sink)


def _gla_kernel(qkf_ref, qkb_ref, vf_ref, vb_ref, laf_ref, lab_ref, s0_ref,
                of_ref, ob_ref, sfin_ref, s_ref, *, tb):
    n = pl.program_id(1)
    npairs = tb // PAIR

    @pl.when(n == 0)
    def _():
        s_ref[...] = s0_ref[0]

    r = lax.broadcasted_iota(jnp.int32, (PAIR, PAIR), 0)
    c = lax.broadcasted_iota(jnp.int32, (PAIR, PAIR), 1)
    same = (r // GLA_CHUNK) == (c // GLA_CHUNK)
    ones_c = jnp.where(same, 1.0, 0.0).astype(BF)
    lane_head = lax.broadcasted_iota(jnp.int32, (PAIR, QK_W), 1) // GLA_DK
    row_chunk = lax.broadcasted_iota(jnp.int32, (PAIR, QK_W), 0) // GLA_CHUNK
    bd_mask = (lax.broadcasted_iota(jnp.int32, (QK_W, V_W), 0) // GLA_DK
               == lax.broadcasted_iota(jnp.int32, (QK_W, V_W), 1) // GLA_DV)

    def stage_decay(d):
        tri_b = jnp.where(d["tri"], 1.0, 0.0).astype(BF)
        la = d["la_ref"][0, d["rows"], :]
        hi = la.astype(BF)
        lo = (la - hi.astype(F32)).astype(BF)
        d["bcum"] = _dot(tri_b, hi) + _dot(tri_b, lo)
        d["blast"] = _dot(ones_c, hi) + _dot(ones_c, lo)

    def stage_scale(d):
        qk = d["qk_ref"][0, d["rows"], :].astype(F32)
        q = qk[:, :QK_W] * (GLA_DK ** -0.5)
        k = qk[:, QK_W:]
        late = row_chunk == (1 if d["forward"] else 0)
        blast = d["blast"]
        other = jnp.concatenate([blast[GLA_CHUNK:], blast[:GLA_CHUNK]], axis=0)
        d_other = jnp.exp(other)
        q_in = q * jnp.exp(d["bcum"])
        k_st = k * jnp.exp(blast - d["bcum"])
        d["q_in"] = q_in.astype(BF)
        d["k_in"] = (k * jnp.exp(-d["bcum"])).astype(BF)
        d["k_st"] = k_st.astype(BF)
        d["q_state"] = jnp.where(late, q_in * d_other, q_in).astype(BF)
        d["k_state_t"] = jnp.where(late, k_st, k_st * d_other).T.astype(BF)
        total_t = (blast[0:SUBLANES] + other[0:SUBLANES]).T
        d["decay"] = jnp.exp(total_t[:, 0:1])

    def stage_state_inc(d):
        v = d["v_ref"][0, d["rows"], :]
        d["u"] = jnp.where(bd_mask, _dot(d["k_state_t"], v), 0.0)

    def stage_scores(d):
        q_in = d["q_in"]
        qs = jnp.concatenate(
            [jnp.where(lane_head == hh, q_in, jnp.zeros_like(q_in)) for hh in range(GLA_HEADS)], axis=0)
        a2 = _dot_nt(qs, jnp.concatenate([d["k_in"], d["k_st"]], axis=0))
        cross = ((r >= GLA_CHUNK) & (c < GLA_CHUNK)) if d["forward"] else ((r < GLA_CHUNK) & (c >= GLA_CHUNK))
        tri4 = jnp.concatenate([d["tri"]] * GLA_HEADS, axis=0)
        cross4 = jnp.concatenate([cross] * GLA_HEADS, axis=0)
        d["a"] = jnp.where(tri4, a2[:, :PAIR], jnp.where(cross4, a2[:, PAIR:], 0.0)).astype(BF)

    def stage_intra(d):
        v = d["v_ref"][0, d["rows"], :]
        a = d["a"]
        d["o_intra"] = jnp.concatenate(
            [_dot(a[PAIR * hh:PAIR * (hh + 1)], v[:, GLA_DV * hh:GLA_DV * (hh + 1)])
             for hh in range(GLA_HEADS)], axis=1)

    def stage_scan(d):
        st = s_ref[d["sidx"]]
        d["o_ref"][0, d["rows"], :] = d["o_intra"] + _dot(d["q_state"], st.astype(BF))
        s_ref[d["sidx"]] = st * d["decay"] + d["u"]

    def body(p, carry):
        dirs = [dict(qk_ref=qkf_ref, v_ref=vf_ref, la_ref=laf_ref, o_ref=of_ref, sidx=0,
                     forward=True, start=pl.multiple_of(p * PAIR, PAIR)),
                dict(qk_ref=qkb_ref, v_ref=vb_ref, la_ref=lab_ref, o_ref=ob_ref, sidx=1,
                     forward=False, start=pl.multiple_of((npairs - 1 - p) * PAIR, PAIR))]
        for d in dirs:
            d["rows"] = pl.ds(d["start"], PAIR)
            d["tri"] = jnp.logical_and(same, (c <= r) if d["forward"] else (c >= r))
        for stage in (stage_decay, stage_scale, stage_state_inc, stage_scores, stage_intra, stage_scan):
            for d in dirs:
                stage(d)
        return carry

    lax.fori_loop(0, npairs, body, 0)

    @pl.when(n == pl.num_programs(1) - 1)
    def _():
        sfin_ref[0] = s_ref[...]


def _gla_call(qk, v, la, s0, tb):
    b, l, _ = qk.shape
    nb = l // tb
    fwd = lambda c: pl.BlockSpec((1, tb, c), lambda bi, i: (bi, i, 0))
    bwd = lambda c: pl.BlockSpec((1, tb, c), lambda bi, i: (bi, nb - 1 - i, 0))
    st_spec = pl.BlockSpec((1, 2, QK_W, V_W), lambda bi, i: (bi, 0, 0, 0))
    return pl.pallas_call(
        functools.partial(_gla_kernel, tb=tb),
        grid=(b, nb),
        in_specs=[fwd(2 * QK_W), bwd(2 * QK_W), fwd(V_W), bwd(V_W),
                  pl.BlockSpec((1, tb, QK_W), lambda bi, i: (bi, i, 0)),
                  pl.BlockSpec((1, tb, QK_W), lambda bi, i: (bi, nb - 1 - i, 1)),
                  st_spec],
        out_specs=(fwd(V_W), bwd(V_W), st_spec),
        out_shape=(jax.ShapeDtypeStruct((b, l, V_W), F32),
                   jax.ShapeDtypeStruct((b, l, V_W), F32),
                   jax.ShapeDtypeStruct((b, 2, QK_W, V_W), F32)),
        scratch_shapes=[pltpu.VMEM((2, QK_W, V_W), F32)],
        compiler_params=_params(("parallel", "arbitrary")),
        name="gla",
    )(qk, qk, v, v, la, la, s0)


def _attn_kernel(qt_ref, k_ref, ct_ref, wuv_ref, o_ref, m_ref, acc_ref, ot_ref, s_ref, p_ref, *, tk):
    tq = qt_ref.shape[2]
    nchunks = k_ref.shape[1] // tk
    m_ref[...] = jnp.full(m_ref.shape, NEG_BIG, F32)
    acc_ref[...] = jnp.zeros(acc_ref.shape, F32)

    def score_stage(kstart, hh, slot):
        s = _dot(k_ref[0, pl.ds(kstart, tk), :], qt_ref[0, Q_ABS * hh:Q_ABS * (hh + 1), :])
        s_ref[slot] = s
        return jnp.max(s.reshape(tk // SUBLANES, SUBLANES, tq), axis=0)

    def score_finish(hh, part):
        m_old = m_ref[hh:hh + 1, :]
        m_new = jnp.maximum(m_old, jnp.max(part, axis=0, keepdims=True))
        m_ref[hh:hh + 1, :] = m_new
        return m_new, jnp.exp2(m_old - m_new)

    def value_stage(slot, pslot, m_new):
        p_ref[pslot] = jnp.exp2(s_ref[slot] - m_new).astype(BF)

    ahead = []
    for hh in range(ATTN_SKEW):
        ahead.extend(score_finish(hh, score_stage(0, hh, hh % SCORE_SLOTS)))

    def chunk_body(j, carry):
        pend = [(carry[2 * u], carry[2 * u + 1]) for u in range(ATTN_SKEW)]
        start = pl.multiple_of(j * tk, KEY_TILES[-1])
        start_next = pl.multiple_of(jnp.minimum(j + 1, nchunks - 1) * tk, KEY_TILES[-1])
        for hh in range(MLA_HEADS):
            m_cur, alpha_cur = pend.pop(0)
            nh = (hh + ATTN_SKEW) % MLA_HEADS
            nstart = start if hh + ATTN_SKEW < MLA_HEADS else start_next
            value_stage(hh % SCORE_SLOTS, hh % 2, m_cur)
            part = score_stage(nstart, nh, (hh + ATTN_SKEW) % SCORE_SLOTS)
            cc = ct_ref[0, :, pl.ds(start, tk)]
            acc_ref[hh] = alpha_cur * acc_ref[hh] + _dot(cc, p_ref[hh % 2])
            pend.append(score_finish(nh, part))
        return tuple(v for pair in pend for v in pair)

    lax.fori_loop(0, nchunks, chunk_body, tuple(ahead))
    for hh in range(MLA_HEADS):
        a = acc_ref[hh]
        o_lat = (a[:MLA_KV_LORA] / a[MLA_KV_LORA:MLA_KV_LORA + 1]).astype(BF)
        ot_ref[MLA_DV * hh:MLA_DV * (hh + 1), :] = _dot(wuv_ref[hh], o_lat)
    o_ref[0] = ot_ref[...].T.astype(BF)


def _attn_call(qt, kabs, ct, wuv, tq, tk):
    b, _, l = qt.shape
    lk = kabs.shape[1]
    ct1 = jnp.concatenate([ct, jnp.ones((b, ONES_ROWS, lk), BF)], axis=1)
    vrows = MLA_KV_LORA + ONES_ROWS
    return pl.pallas_call(
        functools.partial(_attn_kernel, tk=tk),
        grid=(b, l // tq),
        in_specs=[pl.BlockSpec((1, MLA_HEADS * Q_ABS, tq), lambda bi, i: (bi, 0, i)),
                  pl.BlockSpec((1, lk, Q_ABS), lambda bi, i: (bi, 0, 0)),
                  pl.BlockSpec((1, vrows, lk), lambda bi, i: (bi, 0, 0)),
                  _const_spec(wuv.shape)],
        out_specs=pl.BlockSpec((1, tq, MLA_HEADS * MLA_DV), lambda bi, i: (bi, i, 0)),
        out_shape=jax.ShapeDtypeStruct((b, l, MLA_HEADS * MLA_DV), BF),
        scratch_shapes=[pltpu.VMEM((MLA_HEADS, tq), F32),
                        pltpu.VMEM((MLA_HEADS, vrows, tq), F32),
                        pltpu.VMEM((MLA_HEADS * MLA_DV, tq), F32),
                        pltpu.VMEM((SCORE_SLOTS, tk, tq), F32),
                        pltpu.VMEM((2, tk, tq), BF)],
        compiler_params=_params(("parallel", "parallel")),
        name="attn",
    )(qt, kabs, ct1, wuv)


def _shift_rows(t, prev_row, next_row):
    n = t.shape[0]
    row = lax.broadcasted_iota(jnp.int32, t.shape, 0)
    down = jnp.where(row == 0, prev_row, pltpu.roll(t, 1, 0))
    up = jnp.where(row == n - 1, next_row, pltpu.roll(t, n - 1, 0))
    return down, up


def _merge_kernel(x_ref, mod_ref, g1_ref, of_ref, ob_ref, oat_ref, u_ref, up_ref, un_ref,
                  wrsg_ref, glag_ref, wbra_ref, wbrb_ref, wbrc_ref, scw_ref, wo_ref, xo_ref):
    i = pl.program_id(1)
    x = x_ref[0]
    m = mod_ref[0]
    d = x.shape[1]
    hb = (_rms(x, g1_ref[...]) * (1.0 + m[1:2]) + m[0:1]).astype(BF)
    rsg = _dot(hb, wrsg_ref[...])
    r_a = rsg[:, :V_W]
    sb = rsg[:, V_W:V_W + SC_W]
    gates = jax.nn.sigmoid(rsg[:, V_W + SC_W:])

    o = of_ref[0] + ob_ref[0]
    gg = glag_ref[...]
    on = jnp.concatenate(
        [_rms(o[:, GLA_DV * hh:GLA_DV * (hh + 1)], gg[:, GLA_DV * hh:GLA_DV * (hh + 1)])
         for hh in range(GLA_HEADS)], axis=1)
    y_a = _dot((on * _silu(r_a)).astype(BF), wbra_ref[...])
    y_b = _dot(oat_ref[0], wbrb_ref[...])

    u = u_ref[0]
    prev_row = jnp.where(i > 0, up_ref[0, HALO - 1:HALO, :], 0.0)
    next_row = jnp.where(i < pl.num_programs(1) - 1, un_ref[0, 0:1, :], 0.0)
    u_dn, u_up = _shift_rows(u, prev_row, next_row)
    w = scw_ref[...]
    conv = u_dn * w[0:1] + u * w[1:2] + u_up * w[2:3]
    y_c = _dot((sb * conv).astype(BF), wbrc_ref[...])

    mix = gates[:, :d] * y_a + gates[:, d:2 * d] * y_b + gates[:, 2 * d:] * y_c
    xo_ref[0] = x + m[2:3] * _dot(mix.astype(BF), wo_ref[...])


def _halo_specs(tm, l, c):
    nb8 = tm // HALO
    last8 = l // HALO - 1
    prev = pl.BlockSpec((1, HALO, c), lambda bi, i: (bi, jnp.maximum(i * nb8 - 1, 0), 0))
    nxt = pl.BlockSpec((1, HALO, c), lambda bi, i: (bi, jnp.minimum((i + 1) * nb8, last8), 0))
    return prev, nxt


def _merge_call(x, mod, o_f, o_b, o_att, u, lw, tm):
    b, l, d = x.shape
    tok = lambda c: pl.BlockSpec((1, tm, c), lambda bi, i: (bi, i, 0))
    up_spec, un_spec = _halo_specs(tm, l, SC_W)
    weights = [lw["wrsg"], lw["glag"], lw["wbra"], lw["wbrb"], lw["wbrc"], lw["scw"], lw["wo"]]
    return pl.pallas_call(
        _merge_kernel,
        grid=(b, l // tm),
        in_specs=[tok(d), pl.BlockSpec((1, 6, d), lambda bi, i: (bi, 0, 0)), _const_spec(lw["g1"].shape),
                  tok(V_W), tok(V_W), tok(MLA_HEADS * MLA_DV), tok(SC_W), up_spec, un_spec]
                 + [_const_spec(w.shape) for w in weights],
        out_specs=tok(d),
        out_shape=jax.ShapeDtypeStruct((b, l, d), F32),
        compiler_params=_params(("parallel", "parallel")),
        name="merge",
    )(x, mod, lw["g1"], o_f, o_b, o_att, u, u, u, *weights)


def _ffn_kernel(x_ref, xp_ref, xn_ref, mod_ref, g2_ref, wg_ref, wu_ref, cw_ref, wd_ref, gf_ref,
                xo_ref, gs_ref, a_ref, *, fc, final):
    i = pl.program_id(1)
    x = x_ref[0]
    m = mod_ref[0]
    tm = x.shape[0]
    nf = wg_ref.shape[1]

    def hmod(t):
        return _rms(t, g2_ref[...]) * (1.0 + m[4:5]) + m[3:4]

    h = hmod(x)
    hb = h.astype(BF)
    h_prev = jnp.where(i > 0, hmod(xp_ref[0]), 0.0)
    h_next = jnp.where(i < pl.num_programs(1) - 1, hmod(xn_ref[0]), 0.0)
    h_ext = jnp.concatenate([h_prev, h, h_next], axis=0).astype(BF)
    nchunks = nf // fc

    def gate_up(ci):
        cols = slice(ci * fc, (ci + 1) * fc)
        return _dot(h_ext, wg_ref[:, cols]), _dot(hb, wu_ref[:, cols])

    ahead = gate_up(0)
    for ci in range(nchunks):
        cols = slice(ci * fc, (ci + 1) * fc)
        g_ext, up = ahead
        if ci + 1 < nchunks:
            ahead = gate_up(ci + 1)
        cw = cw_ref[:, cols]
        gs = gs_ref.at[ci % 2]
        gs[...] = g_ext
        g = (gs[HALO - 1:HALO - 1 + tm, :] * cw[0:1] + gs[HALO:HALO + tm, :] * cw[1:2]
             + gs[HALO + 1:HALO + 1 + tm, :] * cw[2:3] + cw[3:4])
        a_ref[:, cols] = (_silu(g) * up).astype(BF)
    acc = _dot(a_ref[...], wd_ref[...])
    y = x + m[5:6] * acc
    if final:
        y = _rms(y, gf_ref[...])
    xo_ref[0] = y


def _ffn_call(x, mod, lw, gfinal, tm, final):
    b, l, d = x.shape
    nf = lw["wg"].shape[1]
    fc = MXU_TILE
    tok = pl.BlockSpec((1, tm, d), lambda bi, i: (bi, i, 0))
    xp_spec, xn_spec = _halo_specs(tm, l, d)
    weights = [lw["g2"], lw["wg"], lw["wu"], lw["cw"], lw["wd"], gfinal]
    return pl.pallas_call(
        functools.partial(_ffn_kernel, fc=fc, final=final),
        grid=(b, l // tm),
        in_specs=[tok, xp_spec, xn_spec, pl.BlockSpec((1, 6, d), lambda bi, i: (bi, 0, 0))]
                 + [_const_spec(w.shape) for w in weights],
        out_specs=tok,
        out_shape=jax.ShapeDtypeStruct((b, l, d), F32),
        scratch_shapes=[pltpu.VMEM((2, tm + 2 * HALO, fc), F32), pltpu.VMEM((tm, nf), BF)],
        compiler_params=_params(("parallel", "parallel")),
        name="ffn",
    )(x, x, x, mod, *weights)


def _rope_perm():
    idx = np.zeros((MLA_ROPE,), np.int32)
    sgn = np.zeros((MLA_ROPE,), np.float32)
    for ax in range(2):
        for f in range(ROPE_FREQS):
            lo = ax * 2 * ROPE_FREQS + f
            hi = lo + ROPE_FREQS
            idx[lo], sgn[lo] = hi, -1.0
            idx[hi], sgn[hi] = lo, 1.0
    return idx, sgn


def _rope_tables(l, rotate):
    if rotate:
        rows = l // GRID_W
        row = jnp.repeat(jnp.arange(rows, dtype=F32), GRID_W)
        col = jnp.tile(jnp.arange(GRID_W, dtype=F32), rows)
        inv = ROPE_THETA ** (-jnp.arange(ROPE_FREQS, dtype=F32) / ROPE_FREQS)
        ang = jnp.stack([row[:, None] * inv, col[:, None] * inv], axis=1)
        cos, sin = jnp.cos(ang), jnp.sin(ang)
        expand = lambda t: jnp.broadcast_to(t[:, :, None, :], (l, 2, 2, ROPE_FREQS)).reshape(l, MLA_ROPE)
        cos32, sin32 = expand(cos), expand(sin)
    else:
        cos32, sin32 = jnp.ones((l, MLA_ROPE), F32), jnp.zeros((l, MLA_ROPE), F32)
    cosq = jnp.tile(cos32, (1, MLA_HEADS)).T
    sinq = jnp.tile(sin32, (1, MLA_HEADS)).T
    pad = lambda t: jnp.pad(t, ((0, 0), (0, LANES - MLA_ROPE)))
    return cosq, sinq, pad(cos32), pad(sin32)


def _layer_weights(l, p):
    d = p["w_in"].shape[1]
    w_in = p["w_in"][l]
    sizes = (QK_W, QK_W, V_W, V_W, 2 * GLA_RANK, MLA_Q_LORA, MLA_KV_LORA, MLA_ROPE, SC_W, SC_W, SC_W,
             N_BRANCH * d)
    offs = np.concatenate([[0], np.cumsum(sizes)])
    col = lambda a, b: w_in[:, int(offs[a]):int(offs[b])]
    idx, sgn = _rope_perm()
    w_kr = col(7, 8)
    w_kr_rot = w_kr[:, idx] * sgn
    zeros = lambda n: jnp.zeros((d, n), F32)
    wgk = jnp.zeros((LANES, 2 * QK_W), F32)
    wgk = wgk.at[MLA_ROPE:MLA_ROPE + GLA_RANK, :QK_W].set(p["w_gk2"][l, 0])
    wgk = wgk.at[MLA_ROPE + GLA_RANK:MLA_ROPE + 2 * GLA_RANK, QK_W:].set(p["w_gk2"][l, 1])
    w_uq = p["w_uq"][l].reshape(MLA_Q_LORA, MLA_HEADS, MLA_NOPE + MLA_ROPE)
    w_uq_rope = w_uq[:, :, MLA_NOPE:]
    w_ukv = p["w_ukv"][l].reshape(MLA_KV_LORA, MLA_HEADS, MLA_NOPE + MLA_DV)
    nf = p["w_ffn_gate"].shape[2]
    cw = jnp.concatenate([p["ffn_conv_w"][l], p["ffn_conv_b"][l][None], jnp.zeros((SUBLANES - 4, nf), F32)],
                         axis=0)
    scw = jnp.concatenate([p["sc_w"][l], jnp.zeros((SUBLANES - 3, SC_W), F32)], axis=0)
    bf = lambda t: t.astype(BF)
    return {
        "g1": p["norm1_g"][l][None],
        "wqkv": bf(col(0, 3)),
        "wsm1": bf(jnp.concatenate([w_kr, col(4, 5), zeros(LANES - MLA_ROPE - 2 * GLA_RANK)], axis=1)),
        "wsm2": bf(jnp.concatenate([w_kr_rot, zeros(LANES - MLA_ROPE)], axis=1)),
        "wgk": bf(wgk),
        "bgk": p["b_gk2"][l].reshape(1, 2 * QK_W),
        "wcq": bf(col(5, 6)),
        "gq": p["mla_q_norm_g"][l][None],
        "wuqn": bf(w_uq[:, :, :MLA_NOPE].reshape(MLA_Q_LORA, -1).T),
        "wuk": bf(jnp.transpose(w_ukv[:, :, :MLA_NOPE], (1, 0, 2))),
        "wuqr": bf(w_uq_rope.reshape(MLA_Q_LORA, -1).T),
        "wuqrr": bf((w_uq_rope[:, :, idx] * sgn).reshape(MLA_Q_LORA, -1).T),
        "wckv": bf(col(6, 7)),
        "gkv": p["mla_kv_norm_g"][l][None],
        "ws": bf(col(9, 11)),
        "wuv": bf(jnp.transpose(w_ukv[:, :, MLA_NOPE:], (1, 2, 0))),
        "wrsg": bf(jnp.concatenate([col(3, 4), col(8, 9), col(11, 12)], axis=1)),
        "glag": p["gla_norm_g"][l][None],
        "wbra": bf(p["w_br_a"][l]),
        "wbrb": bf(p["w_br_b"][l]),
        "wbrc": bf(p["w_br_c"][l]),
        "scw": scw,
        "wo": bf(p["w_o"][l]),
        "g2": p["norm2_g"][l][None],
        "wg": bf(p["w_ffn_gate"][l]),
        "wu": bf(p["w_ffn_up"][l]),
        "cw": cw,
        "wd": bf(p["w_ffn_down"][l]),
    }


def _pick(n, options):
    for o in options:
        if n % o == 0:
            return o
    raise ValueError(f"no tile in {options} divides {n}")


def kernel(x, c, ctx, c_ctx, w_ada, b_ada, norm1_g, w_in, w_gk2, b_gk2, gla_norm_g, mla_q_norm_g, w_uq,
           mla_kv_norm_g, w_ukv, sc_w, w_br_a, w_br_b, w_br_c, w_o, norm2_g, w_ffn_gate, w_ffn_up,
           ffn_conv_w, ffn_conv_b, w_ffn_down, final_norm_g):
    p = dict(w_in=w_in, w_gk2=w_gk2, b_gk2=b_gk2, gla_norm_g=gla_norm_g, mla_q_norm_g=mla_q_norm_g,
             w_uq=w_uq, mla_kv_norm_g=mla_kv_norm_g, w_ukv=w_ukv, sc_w=sc_w, w_br_a=w_br_a,
             w_br_b=w_br_b, w_br_c=w_br_c, w_o=w_o, norm1_g=norm1_g, norm2_g=norm2_g,
             w_ffn_gate=w_ffn_gate, w_ffn_up=w_ffn_up, ffn_conv_w=ffn_conv_w, ffn_conv_b=ffn_conv_b,
             w_ffn_down=w_ffn_down)
    b, l, d = x.shape
    lc = ctx.shape[1]
    depth = w_in.shape[0]
    assert l % PAIR == 0 and lc % PAIR == 0 and l % GRID_W == 0

    tiles = (512, 256, 128)
    tm, tm_c = _pick(l, tiles), _pick(lc, tiles)
    tb, tb_c = _pick(l, (1024,) + tiles), _pick(lc, tiles)
    tq, tq_c = _pick(l, tiles), _pick(lc, tiles)
    tk, tk_c = _pick(l + lc, KEY_TILES), _pick(lc, KEY_TILES)

    rows = -(-(b + 1) // SUBLANES) * SUBLANES
    cc = jnp.concatenate([c, c_ctx[None], jnp.zeros((rows - b - 1, d), F32)], axis=0)
    mod_all = _mod_call(cc, w_ada, b_ada)
    tabs = _rope_tables(l, True)
    tabs_c = _rope_tables(lc, False)
    gfinal = final_norm_g[None]
    zero_state = jnp.zeros((b, 2, QK_W, V_W), F32)

    xc = ctx
    for li in range(depth):
        last = li == depth - 1
        lw = _layer_weights(li, p)
        mod = mod_all[li, :b].reshape(b, 6, d)
        mod_c = jnp.broadcast_to(mod_all[li, b].reshape(1, 6, d), (b, 6, d))

        qk_c, v_c, la_c, qt_c, kabs_c, ct_c, u_c = _proj_call(xc, mod_c, lw, tabs_c, tm_c)
        of_c, ob_c, s_ctx = _gla_call(qk_c, v_c, la_c, zero_state, tb_c)

        qk, v, la, qt, kabs, ct, u = _proj_call(x, mod, lw, tabs, tm)
        o_f, o_b, _ = _gla_call(qk, v, la, s_ctx, tb)
        k_all = jnp.concatenate([kabs, kabs_c], axis=1)
        ct_all = jnp.concatenate([ct, ct_c], axis=2)
        o_att = _attn_call(qt, k_all, ct_all, lw["wuv"], tq, tk)
        x = _merge_call(x, mod, o_f, o_b, o_att, u, lw, tm)
        if not last:
            o_att_c = _attn_call(qt_c, kabs_c, ct_c, lw["wuv"], tq_c, tk_c)
            xc = _merge_call(xc, mod_c, of_c, ob_c, o_att_c, u_c, lw, tm_c)
        x = _ffn_call(x, mod, lw, gfinal, tm, last)
        if not last:
            xc = _ffn_call(xc, mod_c, lw, gfinal, tm_c, False)
    return x
```

```python
import functools

import numpy as np
import jax
import jax.numpy as jnp
from jax import lax
from jax.experimental import pallas as pl
from jax.experimental.pallas import tpu as pltpu

F32 = jnp.float32
BF = jnp.bfloat16

LANES = 128
SUBLANES = 8
MXU_TILE = 256
VMEM_LIMIT = 56 * 1024 * 1024

GRID_W = 64
GLA_HEADS = 4
GLA_DK = 64
GLA_DV = 128
GLA_RANK = 16
GLA_GATE_NORM = 16.0
GLA_CHUNK = 64
MLA_HEADS = 8
MLA_Q_LORA = 256
MLA_KV_LORA = 128
MLA_NOPE = 64
MLA_ROPE = 32
MLA_DV = 64
ROPE_THETA = 10000.0
ROPE_FREQS = MLA_ROPE // 4
SC_W = 512
NORM_EPS = 1e-6
N_BRANCH = 3

QK_W = GLA_HEADS * GLA_DK
V_W = GLA_HEADS * GLA_DV
Q_ABS = MLA_KV_LORA + MLA_ROPE
PAIR = 2 * GLA_CHUNK
HALO = SUBLANES
NEG_BIG = float(np.finfo(np.float32).min)
ONES_ROWS = 16
LOG2E = 1.4426950408889634
KEY_TILES = (768, 512, 256)
ATTN_SKEW = 2
SCORE_SLOTS = 4
assert MLA_HEADS % SCORE_SLOTS == 0 and SCORE_SLOTS > ATTN_SKEW


def _dot(a, b):
    return jnp.dot(a, b, preferred_element_type=F32)


def _dot_nt(a, b):
    return lax.dot_general(a, b, (((1,), (1,)), ((), ())), preferred_element_type=F32)


def _rms(x, g):
    return x * lax.rsqrt(jnp.mean(x * x, axis=-1, keepdims=True) + NORM_EPS) * g


def _silu(x):
    return x * jax.nn.sigmoid(x)


def _const_spec(shape):
    nd = len(shape)
    return pl.BlockSpec(shape, lambda *_: (0,) * nd, pipeline_mode=pl.Buffered(1))


def _params(sem):
    return pltpu.CompilerParams(dimension_semantics=sem, vmem_limit_bytes=VMEM_LIMIT)


def _mod_kernel(c_ref, w_ref, b_ref, o_ref):
    a = _silu(c_ref[...]).astype(BF)
    o_ref[0] = _dot(a, w_ref[0].astype(BF)) + b_ref[0]


def _mod_call(cc, w_ada, b_ada):
    nl, d, n = w_ada.shape
    rows = cc.shape[0]
    tn = 4 * MXU_TILE
    return pl.pallas_call(
        _mod_kernel,
        grid=(nl, n // tn),
        in_specs=[
            pl.BlockSpec((rows, d), lambda l, j: (0, 0)),
            pl.BlockSpec((1, d, tn), lambda l, j: (l, 0, j)),
            pl.BlockSpec((1, 1, tn), lambda l, j: (l, 0, j)),
        ],
        out_specs=pl.BlockSpec((1, rows, tn), lambda l, j: (l, 0, j)),
        out_shape=jax.ShapeDtypeStruct((nl, rows, n), F32),
        compiler_params=_params(("parallel", "parallel")),
        name="adaln_mod",
    )(cc, w_ada, b_ada.reshape(nl, 1, n))


def _proj_kernel(x_ref, mod_ref, g1_ref, wqkv_ref, wsm1_ref, wsm2_ref, wgk_ref, bgk_ref,
                 wcq_ref, gq_ref, wuqn_ref, wuk_ref, wuqr_ref, wuqrr_ref, wckv_ref, gkv_ref,
                 ws_ref, cosq_ref, sinq_ref, cosk_ref, sink_ref,
                 qk_ref, v_ref, vt_ref, la_ref, qt_ref, kabs_ref, ct_ref, u_ref, h_ref):
    x = x_ref[0]
    m = mod_ref[0]
    h = _rms(x, g1_ref[...]) * (1.0 + m[1:2]) + m[0:1]
    hb = h.astype(BF)
    h_ref[0] = hb

    sm1 = _dot(hb, wsm1_ref[...])
    sm2 = _dot(hb, wsm2_ref[...])
    cq = _dot(hb, wcq_ref[...])
    ckv = _dot(hb, wckv_ref[...])

    qkv = _dot(hb, wqkv_ref[...])

    z = _dot(sm1.astype(BF), wgk_ref[...]) + bgk_ref[...]
    la_ref[0] = jax.nn.log_sigmoid(z) * (1.0 / GLA_GATE_NORM)

    scale = (MLA_NOPE + MLA_ROPE) ** -0.5 * LOG2E
    cqn = _rms(cq, gq_ref[...]).astype(BF)
    qnt = _dot_nt(wuqn_ref[...], cqn).astype(BF)
    qr_t = _dot_nt(wuqr_ref[...], cqn)
    qrr_t = _dot_nt(wuqrr_ref[...], cqn)

    s = _dot(hb, ws_ref[...])

    qk_ref[0] = qkv[:, :2 * QK_W].astype(BF)
    v = qkv[:, 2 * QK_W:]
    v_ref[0] = v.astype(BF)
    vt_ref[0] = v.T.astype(BF)

    qrope_t = (qr_t * cosq_ref[...] + qrr_t * sinq_ref[...]) * scale
    for hh in range(MLA_HEADS):
        qa = _dot(wuk_ref[hh], qnt[MLA_NOPE * hh:MLA_NOPE * (hh + 1), :]) * scale
        qt_ref[0, Q_ABS * hh:Q_ABS * hh + MLA_KV_LORA, :] = qa.astype(BF)
        qt_ref[0, Q_ABS * hh + MLA_KV_LORA:Q_ABS * (hh + 1), :] = (
            qrope_t[MLA_ROPE * hh:MLA_ROPE * (hh + 1), :].astype(BF))

    cn = _rms(ckv, gkv_ref[...])
    kabs_ref[0, :, 0:MLA_KV_LORA] = cn.astype(BF)
    kr = sm1 * cosk_ref[...] + sm2 * sink_ref[...]
    kabs_ref[0, :, MLA_KV_LORA:Q_ABS] = kr[:, 0:MLA_ROPE].astype(BF)
    ct_ref[0] = cn.T.astype(BF)

    u_ref[0] = s[:, :SC_W] * s[:, SC_W:]


def _proj_call(x, mod, lw, tabs, tm):
    b, l, d = x.shape
    cosq, sinq, cosk, sink = tabs
    tok = lambda c: pl.BlockSpec((1, tm, c), lambda bi, i: (bi, i, 0))
    tr = lambda r: pl.BlockSpec((1, r, tm), lambda bi, i: (bi, 0, i))
    weights = [lw["g1"], lw["wqkv"], lw["wsm1"], lw["wsm2"], lw["wgk"], lw["bgk"], lw["wcq"],
               lw["gq"], lw["wuqn"], lw["wuk"], lw["wuqr"], lw["wuqrr"], lw["wckv"], lw["gkv"],
               lw["ws"]]
    in_specs = ([tok(d), pl.BlockSpec((1, 6, d), lambda bi, i: (bi, 0, 0))]
                + [_const_spec(w.shape) for w in weights]
                + [pl.BlockSpec((MLA_HEADS * MLA_ROPE, tm), lambda bi, i: (0, i)),
                   pl.BlockSpec((MLA_HEADS * MLA_ROPE, tm), lambda bi, i: (0, i)),
                   pl.BlockSpec((tm, LANES), lambda bi, i: (i, 0)),
                   pl.BlockSpec((tm, LANES), lambda bi, i: (i, 0))])
    out_shape = (
        jax.ShapeDtypeStruct((b, l, 2 * QK_W), BF),
        jax.ShapeDtypeStruct((b, l, V_W), BF),
        jax.ShapeDtypeStruct((b, V_W, l), BF),
        jax.ShapeDtypeStruct((b, l, 2 * QK_W), F32),
        jax.ShapeDtypeStruct((b, MLA_HEADS * Q_ABS, l), BF),
        jax.ShapeDtypeStruct((b, l, Q_ABS), BF),
        jax.ShapeDtypeStruct((b, MLA_KV_LORA, l), BF),
        jax.ShapeDtypeStruct((b, l, SC_W), F32),
        jax.ShapeDtypeStruct((b, l, d), BF),
    )
    out_specs = (tok(2 * QK_W), tok(V_W), tr(V_W), tok(2 * QK_W), tr(MLA_HEADS * Q_ABS),
                 tok(Q_ABS), tr(MLA_KV_LORA), tok(SC_W), tok(d))
    return pl.pallas_call(
        _proj_kernel,
        grid=(b, l // tm),
        in_specs=in_specs,
        out_specs=out_specs,
        out_shape=out_shape,
        compiler_params=_params(("parallel", "parallel")),
        name="proj",
    )(x, mod, *weights, cosq, sinq, cosk, sink)


def _gla_kernel(qkf_ref, qkb_ref, vf_ref, vb_ref, vtf_ref, vtb_ref, laf_ref, lab_ref, s0_ref,
                of_ref, ob_ref, sfin_ref, s_ref, *, tb):
    n = pl.program_id(1)
    npairs = tb // PAIR

    @pl.when(n == 0)
    def _():
        s_ref[...] = s0_ref[0]

    r = lax.broadcasted_iota(jnp.int32, (PAIR, PAIR), 0)
    c = lax.broadcasted_iota(jnp.int32, (PAIR, PAIR), 1)
    same = (r // GLA_CHUNK) == (c // GLA_CHUNK)
    ones_c = jnp.where(same, 1.0, 0.0).astype(BF)
    lane_head = lax.broadcasted_iota(jnp.int32, (PAIR, QK_W), 1) // GLA_DK
    row_chunk = lax.broadcasted_iota(jnp.int32, (PAIR, QK_W), 0) // GLA_CHUNK
    bd_mask = (lax.broadcasted_iota(jnp.int32, (V_W, QK_W), 0) // GLA_DV
               == lax.broadcasted_iota(jnp.int32, (V_W, QK_W), 1) // GLA_DK)

    def stage_decay(d):
        tri_b = jnp.where(d["tri"], 1.0, 0.0).astype(BF)
        la = d["la_ref"][0, d["rows"], :]
        hi = la.astype(BF)
        lo = (la - hi.astype(F32)).astype(BF)
        d["bcum"] = _dot(tri_b, hi) + _dot(tri_b, lo)
        d["blast"] = _dot(ones_c, hi) + _dot(ones_c, lo)

    def stage_scale(d):
        qk = d["qk_ref"][0, d["rows"], :].astype(F32)
        q = qk[:, :QK_W] * (GLA_DK ** -0.5)
        k = qk[:, QK_W:]
        late = row_chunk == (1 if d["forward"] else 0)
        blast = d["blast"]
        other = jnp.concatenate([blast[GLA_CHUNK:], blast[:GLA_CHUNK]], axis=0)
        d_other = jnp.exp(other)
        q_in = q * jnp.exp(d["bcum"])
        k_st = k * jnp.exp(blast - d["bcum"])
        d["q_in"] = q_in.astype(BF)
        d["k_in"] = (k * jnp.exp(-d["bcum"])).astype(BF)
        d["k_st"] = k_st.astype(BF)
        d["q_state"] = jnp.where(late, q_in * d_other, q_in).astype(BF)
        d["k_state"] = jnp.where(late, k_st, k_st * d_other).astype(BF)
        d["decay"] = jnp.exp(blast[0:1] + other[0:1])

    def stage_state_inc(d):
        vt = d["vt_ref"][0, :, d["rows"]]
        d["ut"] = jnp.where(bd_mask, _dot(vt, d["k_state"]), 0.0)

    def stage_scores(d):
        q_in = d["q_in"]
        qs = jnp.concatenate(
            [jnp.where(lane_head == hh, q_in, jnp.zeros_like(q_in)) for hh in range(GLA_HEADS)], axis=0)
        a2 = _dot_nt(qs, jnp.concatenate([d["k_in"], d["k_st"]], axis=0))
        cross = ((r >= GLA_CHUNK) & (c < GLA_CHUNK)) if d["forward"] else ((r < GLA_CHUNK) & (c >= GLA_CHUNK))
        tri4 = jnp.concatenate([d["tri"]] * GLA_HEADS, axis=0)
        cross4 = jnp.concatenate([cross] * GLA_HEADS, axis=0)
        d["a"] = jnp.where(tri4, a2[:, :PAIR], jnp.where(cross4, a2[:, PAIR:], 0.0)).astype(BF)

    def stage_intra(d):
        v = d["v_ref"][0, d["rows"], :]
        a = d["a"]
        d["o_intra"] = jnp.concatenate(
            [_dot(a[PAIR * hh:PAIR * (hh + 1)], v[:, GLA_DV * hh:GLA_DV * (hh + 1)])
             for hh in range(GLA_HEADS)], axis=1)

    def stage_scan(d):
        st = s_ref[d["sidx"]]
        d["o_ref"][0, d["rows"], :] = d["o_intra"] + _dot_nt(d["q_state"], st.astype(BF))
        s_ref[d["sidx"]] = st * d["decay"] + d["ut"]

    def body(p, carry):
        dirs = [dict(qk_ref=qkf_ref, v_ref=vf_ref, vt_ref=vtf_ref, la_ref=laf_ref, o_ref=of_ref, sidx=0,
                     forward=True, start=pl.multiple_of(p * PAIR, PAIR)),
                dict(qk_ref=qkb_ref, v_ref=vb_ref, vt_ref=vtb_ref, la_ref=lab_ref, o_ref=ob_ref, sidx=1,
                     forward=False, start=pl.multiple_of((npairs - 1 - p) * PAIR, PAIR))]
        for d in dirs:
            d["rows"] = pl.ds(d["start"], PAIR)
            d["tri"] = jnp.logical_and(same, (c <= r) if d["forward"] else (c >= r))
        for stage in (stage_decay, stage_scale, stage_state_inc, stage_scores, stage_intra, stage_scan):
            for d in dirs:
                stage(d)
        return carry

    lax.fori_loop(0, npairs, body, 0)

    @pl.when(n == pl.num_programs(1) - 1)
    def _():
        sfin_ref[0] = s_ref[...]


def _gla_call(qk, v, vt, la, s0, tb):
    b, l, _ = qk.shape
    nb = l // tb
    fwd = lambda c: pl.BlockSpec((1, tb, c), lambda bi, i: (bi, i, 0))
    bwd = lambda c: pl.BlockSpec((1, tb, c), lambda bi, i: (bi, nb - 1 - i, 0))
    st_spec = pl.BlockSpec((1, 2, V_W, QK_W), lambda bi, i: (bi, 0, 0, 0))
    return pl.pallas_call(
        functools.partial(_gla_kernel, tb=tb),
        grid=(b, nb),
        in_specs=[fwd(2 * QK_W), bwd(2 * QK_W), fwd(V_W), bwd(V_W),
                  pl.BlockSpec((1, V_W, tb), lambda bi, i: (bi, 0, i)),
                  pl.BlockSpec((1, V_W, tb), lambda bi, i: (bi, 0, nb - 1 - i)),
                  pl.BlockSpec((1, tb, QK_W), lambda bi, i: (bi, i, 0)),
                  pl.BlockSpec((1, tb, QK_W), lambda bi, i: (bi, nb - 1 - i, 1)),
                  st_spec],
        out_specs=(fwd(V_W), bwd(V_W), st_spec),
        out_shape=(jax.ShapeDtypeStruct((b, l, V_W), F32),
                   jax.ShapeDtypeStruct((b, l, V_W), F32),
                   jax.ShapeDtypeStruct((b, 2, V_W, QK_W), F32)),
        scratch_shapes=[pltpu.VMEM((2, V_W, QK_W), F32)],
        compiler_params=_params(("parallel", "arbitrary")),
        name="gla",
    )(qk, qk, v, v, vt, vt, la, la, s0)


def _attn_kernel(qt_ref, k_ref, ct_ref, wuv_ref, o_ref, m_ref, acc_ref, ot_ref, s_ref, p_ref, *, tk):
    tq = qt_ref.shape[2]
    nchunks = k_ref.shape[1] // tk
    m_ref[...] = jnp.full(m_ref.shape, NEG_BIG, F32)
    acc_ref[...] = jnp.zeros(acc_ref.shape, F32)

    def score_stage(kstart, hh, slot):
        s = _dot(k_ref[0, pl.ds(kstart, tk), :], qt_ref[0, Q_ABS * hh:Q_ABS * (hh + 1), :])
        s_ref[slot] = s
        return jnp.max(s.reshape(tk // SUBLANES, SUBLANES, tq), axis=0)

    def score_finish(hh, part):
        m_old = m_ref[hh:hh + 1, :]
        m_new = jnp.maximum(m_old, jnp.max(part, axis=0, keepdims=True))
        m_ref[hh:hh + 1, :] = m_new
        return m_new, jnp.exp2(m_old - m_new)

    def value_stage(slot, pslot, m_new):
        p_ref[pslot] = jnp.exp2(s_ref[slot] - m_new).astype(BF)

    ahead = []
    for hh in range(ATTN_SKEW):
        ahead.extend(score_finish(hh, score_stage(0, hh, hh % SCORE_SLOTS)))

    def chunk_body(j, carry):
        pend = [(carry[2 * u], carry[2 * u + 1]) for u in range(ATTN_SKEW)]
        start = pl.multiple_of(j * tk, KEY_TILES[-1])
        start_next = pl.multiple_of(jnp.minimum(j + 1, nchunks - 1) * tk, KEY_TILES[-1])
        for hh in range(MLA_HEADS):
            m_cur, alpha_cur = pend.pop(0)
            nh = (hh + ATTN_SKEW) % MLA_HEADS
            nstart = start if hh + ATTN_SKEW < MLA_HEADS else start_next
            value_stage(hh % SCORE_SLOTS, hh % 2, m_cur)
            part = score_stage(nstart, nh, (hh + ATTN_SKEW) % SCORE_SLOTS)
            cc = ct_ref[0, :, pl.ds(start, tk)]
            acc_ref[hh] = alpha_cur * acc_ref[hh] + _dot(cc, p_ref[hh % 2])
            pend.append(score_finish(nh, part))
        return tuple(v for pair in pend for v in pair)

    lax.fori_loop(0, nchunks, chunk_body, tuple(ahead))
    for hh in range(MLA_HEADS):
        a = acc_ref[hh]
        o_lat = (a[:MLA_KV_LORA] / a[MLA_KV_LORA:MLA_KV_LORA + 1]).astype(BF)
        ot_ref[MLA_DV * hh:MLA_DV * (hh + 1), :] = _dot(wuv_ref[hh], o_lat)
    o_ref[0] = ot_ref[...].T.astype(BF)


def _attn_call(qt, kabs, ct, wuv, tq, tk):
    b, _, l = qt.shape
    lk = kabs.shape[1]
    ct1 = jnp.concatenate([ct, jnp.ones((b, ONES_ROWS, lk), BF)], axis=1)
    vrows = MLA_KV_LORA + ONES_ROWS
    return pl.pallas_call(
        functools.partial(_attn_kernel, tk=tk),
        grid=(b, l // tq),
        in_specs=[pl.BlockSpec((1, MLA_HEADS * Q_ABS, tq), lambda bi, i: (bi, 0, i)),
                  pl.BlockSpec((1, lk, Q_ABS), lambda bi, i: (bi, 0, 0)),
                  pl.BlockSpec((1, vrows, lk), lambda bi, i: (bi, 0, 0)),
                  _const_spec(wuv.shape)],
        out_specs=pl.BlockSpec((1, tq, MLA_HEADS * MLA_DV), lambda bi, i: (bi, i, 0)),
        out_shape=jax.ShapeDtypeStruct((b, l, MLA_HEADS * MLA_DV), BF),
        scratch_shapes=[pltpu.VMEM((MLA_HEADS, tq), F32),
                        pltpu.VMEM((MLA_HEADS, vrows, tq), F32),
                        pltpu.VMEM((MLA_HEADS * MLA_DV, tq), F32),
                        pltpu.VMEM((SCORE_SLOTS, tk, tq), F32),
                        pltpu.VMEM((2, tk, tq), BF)],
        compiler_params=_params(("parallel", "parallel")),
        name="attn",
    )(qt, kabs, ct1, wuv)


def _shift_rows(t, prev_row, next_row):
    n = t.shape[0]
    row = lax.broadcasted_iota(jnp.int32, t.shape, 0)
    down = jnp.where(row == 0, prev_row, pltpu.roll(t, 1, 0))
    up = jnp.where(row == n - 1, next_row, pltpu.roll(t, n - 1, 0))
    return down, up


def _merge_kernel(x_ref, h_ref, mod_ref, of_ref, ob_ref, oat_ref, u_ref, up_ref, un_ref,
                  wrsg_ref, glag_ref, wbra_ref, wbrb_ref, wbrc_ref, scw_ref, wo_ref, xo_ref):
    i = pl.program_id(1)
    x = x_ref[0]
    m = mod_ref[0]
    d = x.shape[1]
    rsg = _dot(h_ref[0], wrsg_ref[...])
    r_a = rsg[:, :V_W]
    sb = rsg[:, V_W:V_W + SC_W]
    gates = jax.nn.sigmoid(rsg[:, V_W + SC_W:])

    o = of_ref[0] + ob_ref[0]
    gg = glag_ref[...]
    on = jnp.concatenate(
        [_rms(o[:, GLA_DV * hh:GLA_DV * (hh + 1)], gg[:, GLA_DV * hh:GLA_DV * (hh + 1)])
         for hh in range(GLA_HEADS)], axis=1)
    y_a = _dot((on * _silu(r_a)).astype(BF), wbra_ref[...])
    y_b = _dot(oat_ref[0], wbrb_ref[...])

    u = u_ref[0]
    prev_row = jnp.where(i > 0, up_ref[0, HALO - 1:HALO, :], 0.0)
    next_row = jnp.where(i < pl.num_programs(1) - 1, un_ref[0, 0:1, :], 0.0)
    u_dn, u_up = _shift_rows(u, prev_row, next_row)
    w = scw_ref[...]
    conv = u_dn * w[0:1] + u * w[1:2] + u_up * w[2:3]
    y_c = _dot((sb * conv).astype(BF), wbrc_ref[...])

    mix = gates[:, :d] * y_a + gates[:, d:2 * d] * y_b + gates[:, 2 * d:] * y_c
    xo_ref[0] = x + m[2:3] * _dot(mix.astype(BF), wo_ref[...])


def _halo_specs(tm, l, c):
    nb8 = tm // HALO
    last8 = l // HALO - 1
    prev = pl.BlockSpec((1, HALO, c), lambda bi, i: (bi, jnp.maximum(i * nb8 - 1, 0), 0))
    nxt = pl.BlockSpec((1, HALO, c), lambda bi, i: (bi, jnp.minimum((i + 1) * nb8, last8), 0))
    return prev, nxt


def _merge_call(x, hb, mod, o_f, o_b, o_att, u, lw, tm):
    b, l, d = x.shape
    tok = lambda c: pl.BlockSpec((1, tm, c), lambda bi, i: (bi, i, 0))
    up_spec, un_spec = _halo_specs(tm, l, SC_W)
    weights = [lw["wrsg"], lw["glag"], lw["wbra"], lw["wbrb"], lw["wbrc"], lw["scw"], lw["wo"]]
    return pl.pallas_call(
        _merge_kernel,
        grid=(b, l // tm),
        in_specs=[tok(d), pl.BlockSpec((1, tm, d), lambda bi, i: (bi, i, 0)),
                  pl.BlockSpec((1, 6, d), lambda bi, i: (bi, 0, 0)),
                  tok(V_W), tok(V_W), tok(MLA_HEADS * MLA_DV), tok(SC_W), up_spec, un_spec]
                 + [_const_spec(w.shape) for w in weights],
        out_specs=tok(d),
        out_shape=jax.ShapeDtypeStruct((b, l, d), F32),
        compiler_params=_params(("parallel", "parallel")),
        name="merge",
    )(x, hb, mod, o_f, o_b, o_att, u, u, u, *weights)


def _ffn_kernel(x_ref, xp_ref, xn_ref, mod_ref, g2_ref, wg_ref, wu_ref, cw_ref, wd_ref, gf_ref,
                xo_ref, gs_ref, a_ref, *, fc, final):
    i = pl.program_id(1)
    x = x_ref[0]
    m = mod_ref[0]
    tm = x.shape[0]
    nf = wg_ref.shape[1]

    def hmod(t):
        return _rms(t, g2_ref[...]) * (1.0 + m[4:5]) + m[3:4]

    h = hmod(x)
    hb = h.astype(BF)
    h_prev = jnp.where(i > 0, hmod(xp_ref[0]), 0.0)
    h_next = jnp.where(i < pl.num_programs(1) - 1, hmod(xn_ref[0]), 0.0)
    h_ext = jnp.concatenate([h_prev, h, h_next], axis=0).astype(BF)
    nchunks = nf // fc

    def gate_up(ci):
        cols = slice(ci * fc, (ci + 1) * fc)
        return _dot(h_ext, wg_ref[:, cols]), _dot(hb, wu_ref[:, cols])

    ahead = gate_up(0)
    for ci in range(nchunks):
        cols = slice(ci * fc, (ci + 1) * fc)
        g_ext, up = ahead
        if ci + 1 < nchunks:
            ahead = gate_up(ci + 1)
        cw = cw_ref[:, cols]
        gs = gs_ref.at[ci % 2]
        gs[...] = g_ext
        g = (gs[HALO - 1:HALO - 1 + tm, :] * cw[0:1] + gs[HALO:HALO + tm, :] * cw[1:2]
             + gs[HALO + 1:HALO + 1 + tm, :] * cw[2:3] + cw[3:4])
        a_ref[:, cols] = (_silu(g) * up).astype(BF)
    acc = _dot(a_ref[...], wd_ref[...])
    y = x + m[5:6] * acc
    if final:
        y = _rms(y, gf_ref[...])
    xo_ref[0] = y


def _ffn_call(x, mod, lw, gfinal, tm, final):
    b, l, d = x.shape
    nf = lw["wg"].shape[1]
    fc = MXU_TILE
    tok = pl.BlockSpec((1, tm, d), lambda bi, i: (bi, i, 0))
    xp_spec, xn_spec = _halo_specs(tm, l, d)
    weights = [lw["g2"], lw["wg"], lw["wu"], lw["cw"], lw["wd"], gfinal]
    return pl.pallas_call(
        functools.partial(_ffn_kernel, fc=fc, final=final),
        grid=(b, l // tm),
        in_specs=[tok, xp_spec, xn_spec, pl.BlockSpec((1, 6, d), lambda bi, i: (bi, 0, 0))]
                 + [_const_spec(w.shape) for w in weights],
        out_specs=tok,
        out_shape=jax.ShapeDtypeStruct((b, l, d), F32),
        scratch_shapes=[pltpu.VMEM((2, tm + 2 * HALO, fc), F32), pltpu.VMEM((tm, nf), BF)],
        compiler_params=_params(("parallel", "parallel")),
        name="ffn",
    )(x, x, x, mod, *weights)


def _rope_perm():
    idx = np.zeros((MLA_ROPE,), np.int32)
    sgn = np.zeros((MLA_ROPE,), np.float32)
    for ax in range(2):
        for f in range(ROPE_FREQS):
            lo = ax * 2 * ROPE_FREQS + f
            hi = lo + ROPE_FREQS
            idx[lo], sgn[lo] = hi, -1.0
            idx[hi], sgn[hi] = lo, 1.0
    return idx, sgn


def _rope_tables(l, rotate):
    if rotate:
        rows = l // GRID_W
        row = jnp.repeat(jnp.arange(rows, dtype=F32), GRID_W)
        col = jnp.tile(jnp.arange(GRID_W, dtype=F32), rows)
        inv = ROPE_THETA ** (-jnp.arange(ROPE_FREQS, dtype=F32) / ROPE_FREQS)
        ang = jnp.stack([row[:, None] * inv, col[:, None] * inv], axis=1)
        cos, sin = jnp.cos(ang), jnp.sin(ang)
        expand = lambda t: jnp.broadcast_to(t[:, :, None, :], (l, 2, 2, ROPE_FREQS)).reshape(l, MLA_ROPE)
        cos32, sin32 = expand(cos), expand(sin)
    else:
        cos32, sin32 = jnp.ones((l, MLA_ROPE), F32), jnp.zeros((l, MLA_ROPE), F32)
    cosq = jnp.tile(cos32, (1, MLA_HEADS)).T
    sinq = jnp.tile(sin32, (1, MLA_HEADS)).T
    pad = lambda t: jnp.pad(t, ((0, 0), (0, LANES - MLA_ROPE)))
    return cosq, sinq, pad(cos32), pad(sin32)


def _layer_weights(l, p):
    d = p["w_in"].shape[1]
    w_in = p["w_in"][l]
    sizes = (QK_W, QK_W, V_W, V_W, 2 * GLA_RANK, MLA_Q_LORA, MLA_KV_LORA, MLA_ROPE, SC_W, SC_W, SC_W,
             N_BRANCH * d)
    offs = np.concatenate([[0], np.cumsum(sizes)])
    col = lambda a, b: w_in[:, int(offs[a]):int(offs[b])]
    idx, sgn = _rope_perm()
    w_kr = col(7, 8)
    w_kr_rot = w_kr[:, idx] * sgn
    zeros = lambda n: jnp.zeros((d, n), F32)
    wgk = jnp.zeros((LANES, 2 * QK_W), F32)
    wgk = wgk.at[MLA_ROPE:MLA_ROPE + GLA_RANK, :QK_W].set(p["w_gk2"][l, 0])
    wgk = wgk.at[MLA_ROPE + GLA_RANK:MLA_ROPE + 2 * GLA_RANK, QK_W:].set(p["w_gk2"][l, 1])
    w_uq = p["w_uq"][l].reshape(MLA_Q_LORA, MLA_HEADS, MLA_NOPE + MLA_ROPE)
    w_uq_rope = w_uq[:, :, MLA_NOPE:]
    w_ukv = p["w_ukv"][l].reshape(MLA_KV_LORA, MLA_HEADS, MLA_NOPE + MLA_DV)
    nf = p["w_ffn_gate"].shape[2]
    cw = jnp.concatenate([p["ffn_conv_w"][l], p["ffn_conv_b"][l][None], jnp.zeros((SUBLANES - 4, nf), F32)],
                         axis=0)
    scw = jnp.concatenate([p["sc_w"][l], jnp.zeros((SUBLANES - 3, SC_W), F32)], axis=0)
    bf = lambda t: t.astype(BF)
    return {
        "g1": p["norm1_g"][l][None],
        "wqkv": bf(col(0, 3)),
        "wsm1": bf(jnp.concatenate([w_kr, col(4, 5), zeros(LANES - MLA_ROPE - 2 * GLA_RANK)], axis=1)),
        "wsm2": bf(jnp.concatenate([w_kr_rot, zeros(LANES - MLA_ROPE)], axis=1)),
        "wgk": bf(wgk),
        "bgk": p["b_gk2"][l].reshape(1, 2 * QK_W),
        "wcq": bf(col(5, 6)),
        "gq": p["mla_q_norm_g"][l][None],
        "wuqn": bf(w_uq[:, :, :MLA_NOPE].reshape(MLA_Q_LORA, -1).T),
        "wuk": bf(jnp.transpose(w_ukv[:, :, :MLA_NOPE], (1, 0, 2))),
        "wuqr": bf(w_uq_rope.reshape(MLA_Q_LORA, -1).T),
        "wuqrr": bf((w_uq_rope[:, :, idx] * sgn).reshape(MLA_Q_LORA, -1).T),
        "wckv": bf(col(6, 7)),
        "gkv": p["mla_kv_norm_g"][l][None],
        "ws": bf(col(9, 11)),
        "wuv": bf(jnp.transpose(w_ukv[:, :, MLA_NOPE:], (1, 2, 0))),
        "wrsg": bf(jnp.concatenate([col(3, 4), col(8, 9), col(11, 12)], axis=1)),
        "glag": p["gla_norm_g"][l][None],
        "wbra": bf(p["w_br_a"][l]),
        "wbrb": bf(p["w_br_b"][l]),
        "wbrc": bf(p["w_br_c"][l]),
        "scw": scw,
        "wo": bf(p["w_o"][l]),
        "g2": p["norm2_g"][l][None],
        "wg": bf(p["w_ffn_gate"][l]),
        "wu": bf(p["w_ffn_up"][l]),
        "cw": cw,
        "wd": bf(p["w_ffn_down"][l]),
    }


def _pick(n, options):
    for o in options:
        if n % o == 0:
            return o
    raise ValueError(f"no tile in {options} divides {n}")


def kernel(x, c, ctx, c_ctx, w_ada, b_ada, norm1_g, w_in, w_gk2, b_gk2, gla_norm_g, mla_q_norm_g, w_uq,
           mla_kv_norm_g, w_ukv, sc_w, w_br_a, w_br_b, w_br_c, w_o, norm2_g, w_ffn_gate, w_ffn_up,
           ffn_conv_w, ffn_conv_b, w_ffn_down, final_norm_g):
    p = dict(w_in=w_in, w_gk2=w_gk2, b_gk2=b_gk2, gla_norm_g=gla_norm_g, mla_q_norm_g=mla_q_norm_g,
             w_uq=w_uq, mla_kv_norm_g=mla_kv_norm_g, w_ukv=w_ukv, sc_w=sc_w, w_br_a=w_br_a,
             w_br_b=w_br_b, w_br_c=w_br_c, w_o=w_o, norm1_g=norm1_g, norm2_g=norm2_g,
             w_ffn_gate=w_ffn_gate, w_ffn_up=w_ffn_up, ffn_conv_w=ffn_conv_w, ffn_conv_b=ffn_conv_b,
             w_ffn_down=w_ffn_down)
    b, l, d = x.shape
    lc = ctx.shape[1]
    depth = w_in.shape[0]
    assert l % PAIR == 0 and lc % PAIR == 0 and l % GRID_W == 0

    tiles = (512, 256, 128)
    tm, tm_c = _pick(l, tiles), _pick(lc, tiles)
    tb, tb_c = _pick(l, (1024,) + tiles), _pick(lc, tiles)
    tq, tq_c = _pick(l, tiles), _pick(lc, tiles)
    tk, tk_c = _pick(l + lc, KEY_TILES), _pick(lc, KEY_TILES)

    rows = -(-(b + 1) // SUBLANES) * SUBLANES
    cc = jnp.concatenate([c, c_ctx[None], jnp.zeros((rows - b - 1, d), F32)], axis=0)
    mod_all = _mod_call(cc, w_ada, b_ada)
    tabs = _rope_tables(l, True)
    tabs_c = _rope_tables(lc, False)
    gfinal = final_norm_g[None]
    zero_state = jnp.zeros((b, 2, V_W, QK_W), F32)

    xc = ctx
    for li in range(depth):
        last = li == depth - 1
        lw = _layer_weights(li, p)
        mod = mod_all[li, :b].reshape(b, 6, d)
        mod_c = jnp.broadcast_to(mod_all[li, b].reshape(1, 6, d), (b, 6, d))

        qk_c, v_c, vt_c, la_c, qt_c, kabs_c, ct_c, u_c, hb_c = _proj_call(xc, mod_c, lw, tabs_c, tm_c)
        of_c, ob_c, s_ctx = _gla_call(qk_c, v_c, vt_c, la_c, zero_state, tb_c)

        qk, v, vt, la, qt, kabs, ct, u, hb = _proj_call(x, mod, lw, tabs, tm)
        o_f, o_b, _ = _gla_call(qk, v, vt, la, s_ctx, tb)
        k_all = jnp.concatenate([kabs, kabs_c], axis=1)
        ct_all = jnp.concatenate([ct, ct_c], axis=2)
        o_att = _attn_call(qt, k_all, ct_all, lw["wuv"], tq, tk)
        x = _merge_call(x, hb, mod, o_f, o_b, o_att, u, lw, tm)
        if not last:
            o_att_c = _attn_call(qt_c, kabs_c, ct_c, lw["wuv"], tq_c, tk_c)
            xc = _merge_call(xc, hb_c, mod_c, of_c, ob_c, o_att_c, u_c, lw, tm_c)
        x = _ffn_call(x, mod, lw, gfinal, tm, last)
        if not last:
            xc = _ffn_call(xc, mod_c, lw, gfinal, tm_c, False)
    return x
```

```python
import functools

import numpy as np
import jax
import jax.numpy as jnp
from jax import lax
from jax.experimental import pallas as pl
from jax.experimental.pallas import tpu as pltpu

F32 = jnp.float32
BF = jnp.bfloat16

LANES = 128
SUBLANES = 8
MXU_TILE = 256
VMEM_LIMIT = 56 * 1024 * 1024

GRID_W = 64
GLA_HEADS = 4
GLA_DK = 64
GLA_DV = 128
GLA_RANK = 16
GLA_GATE_NORM = 16.0
GLA_CHUNK = 64
MLA_HEADS = 8
MLA_Q_LORA = 256
MLA_KV_LORA = 128
MLA_NOPE = 64
MLA_ROPE = 32
MLA_DV = 64
ROPE_THETA = 10000.0
ROPE_FREQS = MLA_ROPE // 4
SC_W = 512
NORM_EPS = 1e-6
N_BRANCH = 3

QK_W = GLA_HEADS * GLA_DK
V_W = GLA_HEADS * GLA_DV
Q_ABS = MLA_KV_LORA + MLA_ROPE
PAIR = 2 * GLA_CHUNK
HALO = SUBLANES
NEG_BIG = float(np.finfo(np.float32).min)
ONES_ROWS = 16
LOG2E = 1.4426950408889634
KEY_TILES = (768, 512, 256)
ATTN_SKEW = 2
SCORE_SLOTS = 4
assert MLA_HEADS % SCORE_SLOTS == 0 and SCORE_SLOTS > ATTN_SKEW


def _dot(a, b):
    return jnp.dot(a, b, preferred_element_type=F32)


def _dot_nt(a, b):
    return lax.dot_general(a, b, (((1,), (1,)), ((), ())), preferred_element_type=F32)


def _rms(x, g):
    return x * lax.rsqrt(jnp.mean(x * x, axis=-1, keepdims=True) + NORM_EPS) * g


def _silu(x):
    return x * jax.nn.sigmoid(x)


def _const_spec(shape):
    nd = len(shape)
    return pl.BlockSpec(shape, lambda *_: (0,) * nd, pipeline_mode=pl.Buffered(1))


def _params(sem):
    return pltpu.CompilerParams(dimension_semantics=sem, vmem_limit_bytes=VMEM_LIMIT)


def _mod_kernel(c_ref, w_ref, b_ref, o_ref):
    a = _silu(c_ref[...]).astype(BF)
    o_ref[0] = _dot(a, w_ref[0].astype(BF)) + b_ref[0]


def _mod_call(cc, w_ada, b_ada):
    nl, d, n = w_ada.shape
    rows = cc.shape[0]
    tn = 4 * MXU_TILE
    return pl.pallas_call(
        _mod_kernel,
        grid=(nl, n // tn),
        in_specs=[
            pl.BlockSpec((rows, d), lambda l, j: (0, 0)),
            pl.BlockSpec((1, d, tn), lambda l, j: (l, 0, j)),
            pl.BlockSpec((1, 1, tn), lambda l, j: (l, 0, j)),
        ],
        out_specs=pl.BlockSpec((1, rows, tn), lambda l, j: (l, 0, j)),
        out_shape=jax.ShapeDtypeStruct((nl, rows, n), F32),
        compiler_params=_params(("parallel", "parallel")),
        name="adaln_mod",
    )(cc, w_ada, b_ada.reshape(nl, 1, n))


def _proj_kernel(x_ref, mod_ref, g1_ref, wqkv_ref, wsm1_ref, wsm2_ref, wgk_ref, bgk_ref,
                 wcq_ref, gq_ref, wuqn_ref, wuk_ref, wuqr_ref, wuqrr_ref, wckv_ref, gkv_ref,
                 ws_ref, cosq_ref, sinq_ref, cosk_ref, sink_ref,
                 qk_ref, v_ref, vt_ref, la_ref, qt_ref, kabs_ref, ct_ref, u_ref, h_ref):
    x = x_ref[0]
    m = mod_ref[0]
    h = _rms(x, g1_ref[...]) * (1.0 + m[1:2]) + m[0:1]
    hb = h.astype(BF)
    h_ref[0] = hb

    sm1 = _dot(hb, wsm1_ref[...])
    sm2 = _dot(hb, wsm2_ref[...])
    cq = _dot(hb, wcq_ref[...])
    ckv = _dot(hb, wckv_ref[...])

    qkv = _dot(hb, wqkv_ref[...])

    z = _dot(sm1.astype(BF), wgk_ref[...]) + bgk_ref[...]
    la_ref[0] = jax.nn.log_sigmoid(z) * (1.0 / GLA_GATE_NORM)

    scale = (MLA_NOPE + MLA_ROPE) ** -0.5 * LOG2E
    cqn = _rms(cq, gq_ref[...]).astype(BF)
    qnt = _dot_nt(wuqn_ref[...], cqn).astype(BF)
    qr_t = _dot_nt(wuqr_ref[...], cqn)
    qrr_t = _dot_nt(wuqrr_ref[...], cqn)

    s = _dot(hb, ws_ref[...])

    qk_ref[0] = qkv[:, :2 * QK_W].astype(BF)
    v = qkv[:, 2 * QK_W:]
    v_ref[0] = v.astype(BF)
    vt_ref[0] = v.T.astype(BF)

    qrope_t = (qr_t * cosq_ref[...] + qrr_t * sinq_ref[...]) * scale
    for hh in range(MLA_HEADS):
        qa = _dot(wuk_ref[hh], qnt[MLA_NOPE * hh:MLA_NOPE * (hh + 1), :]) * scale
        qt_ref[0, Q_ABS * hh:Q_ABS * hh + MLA_KV_LORA, :] = qa.astype(BF)
        qt_ref[0, Q_ABS * hh + MLA_KV_LORA:Q_ABS * (hh + 1), :] = (
            qrope_t[MLA_ROPE * hh:MLA_ROPE * (hh + 1), :].astype(BF))

    cn = _rms(ckv, gkv_ref[...])
    kabs_ref[0, :, 0:MLA_KV_LORA] = cn.astype(BF)
    kr = sm1 * cosk_ref[...] + sm2 * sink_ref[...]
    kabs_ref[0, :, MLA_KV_LORA:Q_ABS] = kr[:, 0:MLA_ROPE].astype(BF)
    ct_ref[0, 0:MLA_KV_LORA, :] = cn.T.astype(BF)
    ct_ref[0, MLA_KV_LORA:, :] = jnp.ones((ONES_ROWS, x.shape[0]), BF)

    u_ref[0] = s[:, :SC_W] * s[:, SC_W:]


def _proj_call(x, mod, lw, tabs, tm):
    b, l, d = x.shape
    cosq, sinq, cosk, sink = tabs
    tok = lambda c: pl.BlockSpec((1, tm, c), lambda bi, i: (bi, i, 0))
    tr = lambda r: pl.BlockSpec((1, r, tm), lambda bi, i: (bi, 0, i))
    weights = [lw["g1"], lw["wqkv"], lw["wsm1"], lw["wsm2"], lw["wgk"], lw["bgk"], lw["wcq"],
               lw["gq"], lw["wuqn"], lw["wuk"], lw["wuqr"], lw["wuqrr"], lw["wckv"], lw["gkv"],
               lw["ws"]]
    in_specs = ([tok(d), pl.BlockSpec((1, 6, d), lambda bi, i: (bi, 0, 0))]
                + [_const_spec(w.shape) for w in weights]
                + [pl.BlockSpec((MLA_HEADS * MLA_ROPE, tm), lambda bi, i: (0, i)),
                   pl.BlockSpec((MLA_HEADS * MLA_ROPE, tm), lambda bi, i: (0, i)),
                   pl.BlockSpec((tm, LANES), lambda bi, i: (i, 0)),
                   pl.BlockSpec((tm, LANES), lambda bi, i: (i, 0))])
    out_shape = (
        jax.ShapeDtypeStruct((b, l, 2 * QK_W), BF),
        jax.ShapeDtypeStruct((b, l, V_W), BF),
        jax.ShapeDtypeStruct((b, V_W, l), BF),
        jax.ShapeDtypeStruct((b, l, 2 * QK_W), F32),
        jax.ShapeDtypeStruct((b, MLA_HEADS * Q_ABS, l), BF),
        jax.ShapeDtypeStruct((b, l, Q_ABS), BF),
        jax.ShapeDtypeStruct((b, MLA_KV_LORA + ONES_ROWS, l), BF),
        jax.ShapeDtypeStruct((b, l, SC_W), F32),
        jax.ShapeDtypeStruct((b, l, d), BF),
    )
    out_specs = (tok(2 * QK_W), tok(V_W), tr(V_W), tok(2 * QK_W), tr(MLA_HEADS * Q_ABS),
                 tok(Q_ABS), tr(MLA_KV_LORA + ONES_ROWS), tok(SC_W), tok(d))
    return pl.pallas_call(
        _proj_kernel,
        grid=(b, l // tm),
        in_specs=in_specs,
        out_specs=out_specs,
        out_shape=out_shape,
        compiler_params=_params(("parallel", "parallel")),
        name="proj",
    )(x, mod, *weights, cosq, sinq, cosk, sink)


def _gla_kernel(qkf_ref, qkb_ref, vf_ref, vb_ref, vtf_ref, vtb_ref, laf_ref, lab_ref, s0_ref,
                of_ref, ob_ref, sfin_ref, s_ref, *, tb):
    n = pl.program_id(1)
    npairs = tb // PAIR

    @pl.when(n == 0)
    def _():
        s_ref[...] = s0_ref[0]

    r = lax.broadcasted_iota(jnp.int32, (PAIR, PAIR), 0)
    c = lax.broadcasted_iota(jnp.int32, (PAIR, PAIR), 1)
    same = (r // GLA_CHUNK) == (c // GLA_CHUNK)
    ones_c = jnp.where(same, 1.0, 0.0).astype(BF)
    lane_head = lax.broadcasted_iota(jnp.int32, (PAIR, QK_W), 1) // GLA_DK
    row_chunk = lax.broadcasted_iota(jnp.int32, (PAIR, QK_W), 0) // GLA_CHUNK
    bd_mask = (lax.broadcasted_iota(jnp.int32, (V_W, QK_W), 0) // GLA_DV
               == lax.broadcasted_iota(jnp.int32, (V_W, QK_W), 1) // GLA_DK)

    def stage_decay(d):
        tri_b = jnp.where(d["tri"], 1.0, 0.0).astype(BF)
        la = d["la_ref"][0, d["rows"], :]
        hi = la.astype(BF)
        lo = (la - hi.astype(F32)).astype(BF)
        d["bcum"] = _dot(tri_b, hi) + _dot(tri_b, lo)
        d["blast"] = _dot(ones_c, hi) + _dot(ones_c, lo)

    def stage_scale(d):
        qk = d["qk_ref"][0, d["rows"], :].astype(F32)
        q = qk[:, :QK_W] * (GLA_DK ** -0.5)
        k = qk[:, QK_W:]
        late = row_chunk == (1 if d["forward"] else 0)
        blast = d["blast"]
        other = jnp.concatenate([blast[GLA_CHUNK:], blast[:GLA_CHUNK]], axis=0)
        d_other = jnp.exp(other)
        q_in = q * jnp.exp(d["bcum"])
        k_st = k * jnp.exp(blast - d["bcum"])
        d["q_in"] = q_in.astype(BF)
        d["k_in"] = (k * jnp.exp(-d["bcum"])).astype(BF)
        d["k_st"] = k_st.astype(BF)
        d["q_state"] = jnp.where(late, q_in * d_other, q_in).astype(BF)
        d["k_state"] = jnp.where(late, k_st, k_st * d_other).astype(BF)
        d["decay"] = jnp.exp(blast[0:1] + other[0:1])

    def stage_state_inc(d):
        vt = d["vt_ref"][0, :, d["rows"]]
        d["ut"] = jnp.where(bd_mask, _dot(vt, d["k_state"]), 0.0)

    def stage_scores(d):
        q_in = d["q_in"]
        qs = jnp.concatenate(
            [jnp.where(lane_head == hh, q_in, jnp.zeros_like(q_in)) for hh in range(GLA_HEADS)], axis=0)
        a2 = _dot_nt(qs, jnp.concatenate([d["k_in"], d["k_st"]], axis=0))
        cross = ((r >= GLA_CHUNK) & (c < GLA_CHUNK)) if d["forward"] else ((r < GLA_CHUNK) & (c >= GLA_CHUNK))
        tri4 = jnp.concatenate([d["tri"]] * GLA_HEADS, axis=0)
        cross4 = jnp.concatenate([cross] * GLA_HEADS, axis=0)
        d["a"] = jnp.where(tri4, a2[:, :PAIR], jnp.where(cross4, a2[:, PAIR:], 0.0)).astype(BF)

    def stage_intra(d):
        v = d["v_ref"][0, d["rows"], :]
        a = d["a"]
        d["o_intra"] = jnp.concatenate(
            [_dot(a[PAIR * hh:PAIR * (hh + 1)], v[:, GLA_DV * hh:GLA_DV * (hh + 1)])
             for hh in range(GLA_HEADS)], axis=1)

    def stage_scan(d):
        st = s_ref[d["sidx"]]
        d["o_ref"][0, d["rows"], :] = d["o_intra"] + _dot_nt(d["q_state"], st.astype(BF))
        s_ref[d["sidx"]] = st * d["decay"] + d["ut"]

    def body(p, carry):
        dirs = [dict(qk_ref=qkf_ref, v_ref=vf_ref, vt_ref=vtf_ref, la_ref=laf_ref, o_ref=of_ref, sidx=0,
                     forward=True, start=pl.multiple_of(p * PAIR, PAIR)),
                dict(qk_ref=qkb_ref, v_ref=vb_ref, vt_ref=vtb_ref, la_ref=lab_ref, o_ref=ob_ref, sidx=1,
                     forward=False, start=pl.multiple_of((npairs - 1 - p) * PAIR, PAIR))]
        for d in dirs:
            d["rows"] = pl.ds(d["start"], PAIR)
            d["tri"] = jnp.logical_and(same, (c <= r) if d["forward"] else (c >= r))
        for stage in (stage_decay, stage_scale, stage_state_inc, stage_scores, stage_intra, stage_scan):
            for d in dirs:
                stage(d)
        return carry

    lax.fori_loop(0, npairs, body, 0)

    @pl.when(n == pl.num_programs(1) - 1)
    def _():
        sfin_ref[0] = s_ref[...]


def _gla_call(qk, v, vt, la, s0, tb):
    b, l, _ = qk.shape
    nb = l // tb
    fwd = lambda c: pl.BlockSpec((1, tb, c), lambda bi, i: (bi, i, 0))
    bwd = lambda c: pl.BlockSpec((1, tb, c), lambda bi, i: (bi, nb - 1 - i, 0))
    st_spec = pl.BlockSpec((1, 2, V_W, QK_W), lambda bi, i: (bi, 0, 0, 0))
    return pl.pallas_call(
        functools.partial(_gla_kernel, tb=tb),
        grid=(b, nb),
        in_specs=[fwd(2 * QK_W), bwd(2 * QK_W), fwd(V_W), bwd(V_W),
                  pl.BlockSpec((1, V_W, tb), lambda bi, i: (bi, 0, i)),
                  pl.BlockSpec((1, V_W, tb), lambda bi, i: (bi, 0, nb - 1 - i)),
                  pl.BlockSpec((1, tb, QK_W), lambda bi, i: (bi, i, 0)),
                  pl.BlockSpec((1, tb, QK_W), lambda bi, i: (bi, nb - 1 - i, 1)),
                  st_spec],
        out_specs=(fwd(V_W), bwd(V_W), st_spec),
        out_shape=(jax.ShapeDtypeStruct((b, l, V_W), F32),
                   jax.ShapeDtypeStruct((b, l, V_W), F32),
                   jax.ShapeDtypeStruct((b, 2, V_W, QK_W), F32)),
        scratch_shapes=[pltpu.VMEM((2, V_W, QK_W), F32)],
        compiler_params=_params(("parallel", "arbitrary")),
        name="gla",
    )(qk, qk, v, v, vt, vt, la, la, s0)


def _attn_kernel(qt_ref, k_ref, ct_ref, wuv_ref, o_ref, m_ref, acc_ref, ot_ref, s_ref, p_ref, *, tk):
    tq = qt_ref.shape[2]
    nchunks = k_ref.shape[1] // tk
    m_ref[...] = jnp.full(m_ref.shape, NEG_BIG, F32)
    acc_ref[...] = jnp.zeros(acc_ref.shape, F32)

    def score_stage(kstart, hh, slot):
        s = _dot(k_ref[0, pl.ds(kstart, tk), :], qt_ref[0, Q_ABS * hh:Q_ABS * (hh + 1), :])
        s_ref[slot] = s
        return jnp.max(s.reshape(tk // SUBLANES, SUBLANES, tq), axis=0)

    def score_finish(hh, part):
        m_old = m_ref[hh:hh + 1, :]
        m_new = jnp.maximum(m_old, jnp.max(part, axis=0, keepdims=True))
        m_ref[hh:hh + 1, :] = m_new
        return m_new, jnp.exp2(m_old - m_new)

    def value_stage(slot, pslot, m_new):
        p_ref[pslot] = jnp.exp2(s_ref[slot] - m_new).astype(BF)

    ahead = []
    for hh in range(ATTN_SKEW):
        ahead.extend(score_finish(hh, score_stage(0, hh, hh % SCORE_SLOTS)))

    def chunk_body(j, carry):
        pend = [(carry[2 * u], carry[2 * u + 1]) for u in range(ATTN_SKEW)]
        start = pl.multiple_of(j * tk, KEY_TILES[-1])
        start_next = pl.multiple_of(jnp.minimum(j + 1, nchunks - 1) * tk, KEY_TILES[-1])
        for hh in range(MLA_HEADS):
            m_cur, alpha_cur = pend.pop(0)
            nh = (hh + ATTN_SKEW) % MLA_HEADS
            nstart = start if hh + ATTN_SKEW < MLA_HEADS else start_next
            value_stage(hh % SCORE_SLOTS, hh % 2, m_cur)
            part = score_stage(nstart, nh, (hh + ATTN_SKEW) % SCORE_SLOTS)
            cc = ct_ref[0, :, pl.ds(start, tk)]
            acc_ref[hh] = alpha_cur * acc_ref[hh] + _dot(cc, p_ref[hh % 2])
            pend.append(score_finish(nh, part))
        return tuple(v for pair in pend for v in pair)

    lax.fori_loop(0, nchunks, chunk_body, tuple(ahead))
    for hh in range(MLA_HEADS):
        a = acc_ref[hh]
        o_lat = (a[:MLA_KV_LORA] / a[MLA_KV_LORA:MLA_KV_LORA + 1]).astype(BF)
        ot_ref[MLA_DV * hh:MLA_DV * (hh + 1), :] = _dot(wuv_ref[hh], o_lat)
    o_ref[0] = ot_ref[...].T.astype(BF)


def _attn_call(qt, kabs, ct, wuv, tq, tk):
    b, _, l = qt.shape
    lk = kabs.shape[1]
    ct1 = ct
    vrows = MLA_KV_LORA + ONES_ROWS
    return pl.pallas_call(
        functools.partial(_attn_kernel, tk=tk),
        grid=(b, l // tq),
        in_specs=[pl.BlockSpec((1, MLA_HEADS * Q_ABS, tq), lambda bi, i: (bi, 0, i)),
                  pl.BlockSpec((1, lk, Q_ABS), lambda bi, i: (bi, 0, 0)),
                  pl.BlockSpec((1, vrows, lk), lambda bi, i: (bi, 0, 0)),
                  _const_spec(wuv.shape)],
        out_specs=pl.BlockSpec((1, tq, MLA_HEADS * MLA_DV), lambda bi, i: (bi, i, 0)),
        out_shape=jax.ShapeDtypeStruct((b, l, MLA_HEADS * MLA_DV), BF),
        scratch_shapes=[pltpu.VMEM((MLA_HEADS, tq), F32),
                        pltpu.VMEM((MLA_HEADS, vrows, tq), F32),
                        pltpu.VMEM((MLA_HEADS * MLA_DV, tq), F32),
                        pltpu.VMEM((SCORE_SLOTS, tk, tq), F32),
                        pltpu.VMEM((2, tk, tq), BF)],
        compiler_params=_params(("parallel", "parallel")),
        name="attn",
    )(qt, kabs, ct1, wuv)


def _shift_rows(t, prev_row, next_row):
    n = t.shape[0]
    row = lax.broadcasted_iota(jnp.int32, t.shape, 0)
    down = jnp.where(row == 0, prev_row, pltpu.roll(t, 1, 0))
    up = jnp.where(row == n - 1, next_row, pltpu.roll(t, n - 1, 0))
    return down, up


def _merge_kernel(x_ref, h_ref, mod_ref, of_ref, ob_ref, oat_ref, u_ref, up_ref, un_ref,
                  wrsg_ref, glag_ref, wbra_ref, wbrb_ref, wbrc_ref, scw_ref, wo_ref, xo_ref):
    i = pl.program_id(1)
    x = x_ref[0]
    m = mod_ref[0]
    d = x.shape[1]
    rsg = _dot(h_ref[0], wrsg_ref[...])
    r_a = rsg[:, :V_W]
    sb = rsg[:, V_W:V_W + SC_W]
    gates = jax.nn.sigmoid(rsg[:, V_W + SC_W:])

    o = of_ref[0] + ob_ref[0]
    gg = glag_ref[...]
    on = jnp.concatenate(
        [_rms(o[:, GLA_DV * hh:GLA_DV * (hh + 1)], gg[:, GLA_DV * hh:GLA_DV * (hh + 1)])
         for hh in range(GLA_HEADS)], axis=1)
    y_a = _dot((on * _silu(r_a)).astype(BF), wbra_ref[...])
    y_b = _dot(oat_ref[0], wbrb_ref[...])

    u = u_ref[0]
    prev_row = jnp.where(i > 0, up_ref[0, HALO - 1:HALO, :], 0.0)
    next_row = jnp.where(i < pl.num_programs(1) - 1, un_ref[0, 0:1, :], 0.0)
    u_dn, u_up = _shift_rows(u, prev_row, next_row)
    w = scw_ref[...]
    conv = u_dn * w[0:1] + u * w[1:2] + u_up * w[2:3]
    y_c = _dot((sb * conv).astype(BF), wbrc_ref[...])

    mix = gates[:, :d] * y_a + gates[:, d:2 * d] * y_b + gates[:, 2 * d:] * y_c
    xo_ref[0] = x + m[2:3] * _dot(mix.astype(BF), wo_ref[...])


def _halo_specs(tm, l, c):
    nb8 = tm // HALO
    last8 = l // HALO - 1
    prev = pl.BlockSpec((1, HALO, c), lambda bi, i: (bi, jnp.maximum(i * nb8 - 1, 0), 0))
    nxt = pl.BlockSpec((1, HALO, c), lambda bi, i: (bi, jnp.minimum((i + 1) * nb8, last8), 0))
    return prev, nxt


def _merge_call(x, hb, mod, o_f, o_b, o_att, u, lw, tm):
    b, l, d = x.shape
    tok = lambda c: pl.BlockSpec((1, tm, c), lambda bi, i: (bi, i, 0))
    up_spec, un_spec = _halo_specs(tm, l, SC_W)
    weights = [lw["wrsg"], lw["glag"], lw["wbra"], lw["wbrb"], lw["wbrc"], lw["scw"], lw["wo"]]
    return pl.pallas_call(
        _merge_kernel,
        grid=(b, l // tm),
        in_specs=[tok(d), pl.BlockSpec((1, tm, d), lambda bi, i: (bi, i, 0)),
                  pl.BlockSpec((1, 6, d), lambda bi, i: (bi, 0, 0)),
                  tok(V_W), tok(V_W), tok(MLA_HEADS * MLA_DV), tok(SC_W), up_spec, un_spec]
                 + [_const_spec(w.shape) for w in weights],
        out_specs=tok(d),
        out_shape=jax.ShapeDtypeStruct((b, l, d), F32),
        compiler_params=_params(("parallel", "parallel")),
        name="merge",
    )(x, hb, mod, o_f, o_b, o_att, u, u, u, *weights)


def _ffn_kernel(x_ref, xp_ref, xn_ref, mod_ref, g2_ref, wg_ref, wu_ref, cw_ref, wd_ref, gf_ref,
                xo_ref, gs_ref, a_ref, *, fc, final):
    i = pl.program_id(1)
    x = x_ref[0]
    m = mod_ref[0]
    tm = x.shape[0]
    nf = wg_ref.shape[1]

    def hmod(t):
        return _rms(t, g2_ref[...]) * (1.0 + m[4:5]) + m[3:4]

    h = hmod(x)
    hb = h.astype(BF)
    h_prev = jnp.where(i > 0, hmod(xp_ref[0]), 0.0)
    h_next = jnp.where(i < pl.num_programs(1) - 1, hmod(xn_ref[0]), 0.0)
    h_ext = jnp.concatenate([h_prev, h, h_next], axis=0).astype(BF)
    nchunks = nf // fc

    def gate_up(ci):
        cols = slice(ci * fc, (ci + 1) * fc)
        return _dot(h_ext, wg_ref[:, cols]), _dot(hb, wu_ref[:, cols])

    ahead = gate_up(0)
    for ci in range(nchunks):
        cols = slice(ci * fc, (ci + 1) * fc)
        g_ext, up = ahead
        if ci + 1 < nchunks:
            ahead = gate_up(ci + 1)
        cw = cw_ref[:, cols]
        gs = gs_ref.at[ci % 2]
        gs[...] = g_ext
        g = (gs[HALO - 1:HALO - 1 + tm, :] * cw[0:1] + gs[HALO:HALO + tm, :] * cw[1:2]
             + gs[HALO + 1:HALO + 1 + tm, :] * cw[2:3] + cw[3:4])
        a_ref[:, cols] = (_silu(g) * up).astype(BF)
    acc = _dot(a_ref[...], wd_ref[...])
    y = x + m[5:6] * acc
    if final:
        y = _rms(y, gf_ref[...])
    xo_ref[0] = y


def _ffn_call(x, mod, lw, gfinal, tm, final):
    b, l, d = x.shape
    nf = lw["wg"].shape[1]
    fc = MXU_TILE
    tok = pl.BlockSpec((1, tm, d), lambda bi, i: (bi, i, 0))
    xp_spec, xn_spec = _halo_specs(tm, l, d)
    weights = [lw["g2"], lw["wg"], lw["wu"], lw["cw"], lw["wd"], gfinal]
    return pl.pallas_call(
        functools.partial(_ffn_kernel, fc=fc, final=final),
        grid=(b, l // tm),
        in_specs=[tok, xp_spec, xn_spec, pl.BlockSpec((1, 6, d), lambda bi, i: (bi, 0, 0))]
                 + [_const_spec(w.shape) for w in weights],
        out_specs=tok,
        out_shape=jax.ShapeDtypeStruct((b, l, d), F32),
        scratch_shapes=[pltpu.VMEM((2, tm + 2 * HALO, fc), F32), pltpu.VMEM((tm, nf), BF)],
        compiler_params=_params(("parallel", "parallel")),
        name="ffn",
    )(x, x, x, mod, *weights)


def _rope_perm():
    idx = np.zeros((MLA_ROPE,), np.int32)
    sgn = np.zeros((MLA_ROPE,), np.float32)
    for ax in range(2):
        for f in range(ROPE_FREQS):
            lo = ax * 2 * ROPE_FREQS + f
            hi = lo + ROPE_FREQS
            idx[lo], sgn[lo] = hi, -1.0
            idx[hi], sgn[hi] = lo, 1.0
    return idx, sgn


def _rope_tables(l, rotate):
    if rotate:
        rows = l // GRID_W
        row = jnp.repeat(jnp.arange(rows, dtype=F32), GRID_W)
        col = jnp.tile(jnp.arange(GRID_W, dtype=F32), rows)
        inv = ROPE_THETA ** (-jnp.arange(ROPE_FREQS, dtype=F32) / ROPE_FREQS)
        ang = jnp.stack([row[:, None] * inv, col[:, None] * inv], axis=1)
        cos, sin = jnp.cos(ang), jnp.sin(ang)
        expand = lambda t: jnp.broadcast_to(t[:, :, None, :], (l, 2, 2, ROPE_FREQS)).reshape(l, MLA_ROPE)
        cos32, sin32 = expand(cos), expand(sin)
    else:
        cos32, sin32 = jnp.ones((l, MLA_ROPE), F32), jnp.zeros((l, MLA_ROPE), F32)
    cosq = jnp.tile(cos32, (1, MLA_HEADS)).T
    sinq = jnp.tile(sin32, (1, MLA_HEADS)).T
    pad = lambda t: jnp.pad(t, ((0, 0), (0, LANES - MLA_ROPE)))
    return cosq, sinq, pad(cos32), pad(sin32)


def _layer_weights(l, p):
    d = p["w_in"].shape[1]
    w_in = p["w_in"][l]
    sizes = (QK_W, QK_W, V_W, V_W, 2 * GLA_RANK, MLA_Q_LORA, MLA_KV_LORA, MLA_ROPE, SC_W, SC_W, SC_W,
             N_BRANCH * d)
    offs = np.concatenate([[0], np.cumsum(sizes)])
    col = lambda a, b: w_in[:, int(offs[a]):int(offs[b])]
    idx, sgn = _rope_perm()
    w_kr = col(7, 8)
    w_kr_rot = w_kr[:, idx] * sgn
    zeros = lambda n: jnp.zeros((d, n), F32)
    wgk = jnp.zeros((LANES, 2 * QK_W), F32)
    wgk = wgk.at[MLA_ROPE:MLA_ROPE + GLA_RANK, :QK_W].set(p["w_gk2"][l, 0])
    wgk = wgk.at[MLA_ROPE + GLA_RANK:MLA_ROPE + 2 * GLA_RANK, QK_W:].set(p["w_gk2"][l, 1])
    w_uq = p["w_uq"][l].reshape(MLA_Q_LORA, MLA_HEADS, MLA_NOPE + MLA_ROPE)
    w_uq_rope = w_uq[:, :, MLA_NOPE:]
    w_ukv = p["w_ukv"][l].reshape(MLA_KV_LORA, MLA_HEADS, MLA_NOPE + MLA_DV)
    nf = p["w_ffn_gate"].shape[2]
    cw = jnp.concatenate([p["ffn_conv_w"][l], p["ffn_conv_b"][l][None], jnp.zeros((SUBLANES - 4, nf), F32)],
                         axis=0)
    scw = jnp.concatenate([p["sc_w"][l], jnp.zeros((SUBLANES - 3, SC_W), F32)], axis=0)
    bf = lambda t: t.astype(BF)
    return {
        "g1": p["norm1_g"][l][None],
        "wqkv": bf(col(0, 3)),
        "wsm1": bf(jnp.concatenate([w_kr, col(4, 5), zeros(LANES - MLA_ROPE - 2 * GLA_RANK)], axis=1)),
        "wsm2": bf(jnp.concatenate([w_kr_rot, zeros(LANES - MLA_ROPE)], axis=1)),
        "wgk": bf(wgk),
        "bgk": p["b_gk2"][l].reshape(1, 2 * QK_W),
        "wcq": bf(col(5, 6)),
        "gq": p["mla_q_norm_g"][l][None],
        "wuqn": bf(w_uq[:, :, :MLA_NOPE].reshape(MLA_Q_LORA, -1).T),
        "wuk": bf(jnp.transpose(w_ukv[:, :, :MLA_NOPE], (1, 0, 2))),
        "wuqr": bf(w_uq_rope.reshape(MLA_Q_LORA, -1).T),
        "wuqrr": bf((w_uq_rope[:, :, idx] * sgn).reshape(MLA_Q_LORA, -1).T),
        "wckv": bf(col(6, 7)),
        "gkv": p["mla_kv_norm_g"][l][None],
        "ws": bf(col(9, 11)),
        "wuv": bf(jnp.transpose(w_ukv[:, :, MLA_NOPE:], (1, 2, 0))),
        "wrsg": bf(jnp.concatenate([col(3, 4), col(8, 9), col(11, 12)], axis=1)),
        "glag": p["gla_norm_g"][l][None],
        "wbra": bf(p["w_br_a"][l]),
        "wbrb": bf(p["w_br_b"][l]),
        "wbrc": bf(p["w_br_c"][l]),
        "scw": scw,
        "wo": bf(p["w_o"][l]),
        "g2": p["norm2_g"][l][None],
        "wg": bf(p["w_ffn_gate"][l]),
        "wu": bf(p["w_ffn_up"][l]),
        "cw": cw,
        "wd": bf(p["w_ffn_down"][l]),
    }


def _pick(n, options):
    for o in options:
        if n % o == 0:
            return o
    raise ValueError(f"no tile in {options} divides {n}")


def kernel(x, c, ctx, c_ctx, w_ada, b_ada, norm1_g, w_in, w_gk2, b_gk2, gla_norm_g, mla_q_norm_g, w_uq,
           mla_kv_norm_g, w_ukv, sc_w, w_br_a, w_br_b, w_br_c, w_o, norm2_g, w_ffn_gate, w_ffn_up,
           ffn_conv_w, ffn_conv_b, w_ffn_down, final_norm_g):
    p = dict(w_in=w_in, w_gk2=w_gk2, b_gk2=b_gk2, gla_norm_g=gla_norm_g, mla_q_norm_g=mla_q_norm_g,
             w_uq=w_uq, mla_kv_norm_g=mla_kv_norm_g, w_ukv=w_ukv, sc_w=sc_w, w_br_a=w_br_a,
             w_br_b=w_br_b, w_br_c=w_br_c, w_o=w_o, norm1_g=norm1_g, norm2_g=norm2_g,
             w_ffn_gate=w_ffn_gate, w_ffn_up=w_ffn_up, ffn_conv_w=ffn_conv_w, ffn_conv_b=ffn_conv_b,
             w_ffn_down=w_ffn_down)
    b, l, d = x.shape
    lc = ctx.shape[1]
    depth = w_in.shape[0]
    assert l % PAIR == 0 and lc % PAIR == 0 and l % GRID_W == 0

    tiles = (512, 256, 128)
    tm, tm_c = _pick(l, tiles), _pick(lc, tiles)
    tb, tb_c = _pick(l, (1024,) + tiles), _pick(lc, tiles)
    tq, tq_c = _pick(l, tiles), _pick(lc, tiles)
    tk, tk_c = _pick(l + lc, KEY_TILES), _pick(lc, KEY_TILES)

    rows = -(-(b + 1) // SUBLANES) * SUBLANES
    cc = jnp.concatenate([c, c_ctx[None], jnp.zeros((rows - b - 1, d), F32)], axis=0)
    mod_all = _mod_call(cc, w_ada, b_ada)
    tabs = _rope_tables(l, True)
    tabs_c = _rope_tables(lc, False)
    gfinal = final_norm_g[None]
    zero_state = jnp.zeros((b, 2, V_W, QK_W), F32)

    xc = ctx
    for li in range(depth):
        last = li == depth - 1
        lw = _layer_weights(li, p)
        mod = mod_all[li, :b].reshape(b, 6, d)
        mod_c = jnp.broadcast_to(mod_all[li, b].reshape(1, 6, d), (b, 6, d))

        qk_c, v_c, vt_c, la_c, qt_c, kabs_c, ct_c, u_c, hb_c = _proj_call(xc, mod_c, lw, tabs_c, tm_c)
        of_c, ob_c, s_ctx = _gla_call(qk_c, v_c, vt_c, la_c, zero_state, tb_c)

        qk, v, vt, la, qt, kabs, ct, u, hb = _proj_call(x, mod, lw, tabs, tm)
        o_f, o_b, _ = _gla_call(qk, v, vt, la, s_ctx, tb)
        k_all = jnp.concatenate([kabs, kabs_c], axis=1)
        ct_all = jnp.concatenate([ct, ct_c], axis=2)
        o_att = _attn_call(qt, k_all, ct_all, lw["wuv"], tq, tk)
        x = _merge_call(x, hb, mod, o_f, o_b, o_att, u, lw, tm)
        if not last:
            o_att_c = _attn_call(qt_c, kabs_c, ct_c, lw["wuv"], tq_c, tk_c)
            xc = _merge_call(xc, hb_c, mod_c, of_c, ob_c, o_att_c, u_c, lw, tm_c)
        x = _ffn_call(x, mod, lw, gfinal, tm, last)
        if not last:
            xc = _ffn_call(xc, mod_c, lw, gfinal, tm_c, False)
    return x
```
